```python
import jax, jax.numpy as jnp
from jax import lax
import numpy as np

D_MODEL = 4096
BATCH = 8
SEQ = 2048
DEPTH = 2

GRID_W = 64
CTX_LEN = 256
HEAD_DIM = 128
ATTN_HEADS = D_MODEL // (2 * HEAD_DIM)
ATTN_KV_HEADS = ATTN_HEADS // 4
ATTN_GROUP = ATTN_HEADS // ATTN_KV_HEADS
WINDOW = 128
ATTN_BLOCK = 128
RET_V_DIM = 2 * HEAD_DIM
RET_HEADS = D_MODEL // (2 * RET_V_DIM)
RET_QK_DIM = HEAD_DIM
RET_CHUNK = 128
ATTN_WIDTH = ATTN_HEADS * HEAD_DIM
KV_WIDTH = ATTN_KV_HEADS * HEAD_DIM
RET_QK_WIDTH = RET_HEADS * RET_QK_DIM
RET_V_WIDTH = RET_HEADS * RET_V_DIM
MIX_WIDTH = ATTN_WIDTH + RET_V_WIDTH
IN_WIDTH = ATTN_WIDTH + 2 * KV_WIDTH + 2 * RET_QK_WIDTH + 2 * RET_V_WIDTH
D_FF = 4 * D_MODEL
ROPE_PAIRS = HEAD_DIM // 4
ROPE_BASE = 10000.0
EPS = 1e-6

kernel_name = 'hybrid_dit_swa_retention_block'

F32 = jnp.float32


def rmsnorm(t, g):
    tf = t.astype(F32)
    out = tf * lax.rsqrt(jnp.mean(tf * tf, axis=-1, keepdims=True) + EPS)
    return (out * g.astype(F32)).astype(t.dtype)


def modulate(h, shift, scale):
    return h * (1 + scale) + shift


def rope_2d(rows, length):
    r, cl = jnp.meshgrid(jnp.arange(rows), jnp.arange(GRID_W), indexing='ij')
    r = r.reshape(-1)[:length].astype(F32)
    cl = cl.reshape(-1)[:length].astype(F32)
    inv = ROPE_BASE ** (-jnp.arange(ROPE_PAIRS, dtype=F32) / ROPE_PAIRS)
    ar = r[:, None] * inv
    ac = cl[:, None] * inv
    ang = jnp.concatenate([ar, ar, ac, ac], axis=-1)
    return jnp.cos(ang), jnp.sin(ang)


def apply_rope(t, cos, sin):
    a1, a2, b1, b2 = jnp.split(t, 4, axis=-1)
    rot = jnp.concatenate([-a2, a1, -b2, b1], axis=-1)
    return t * cos.astype(t.dtype) + rot * sin.astype(t.dtype)


def to_heads(t, n_heads):
    b, l, _ = t.shape
    return t.reshape(b, l, n_heads, -1).transpose(0, 2, 1, 3)


def flip(t):
    return jnp.flip(t, axis=2)


def project(h, w_in, rope):
    s0 = ATTN_WIDTH
    s1 = s0 + KV_WIDTH
    s2 = s1 + KV_WIDTH
    s3 = s2 + RET_QK_WIDTH
    s4 = s3 + RET_QK_WIDTH
    s5 = s4 + RET_V_WIDTH
    p = h @ w_in
    aq, ak, av, rq, rk, rv, rg = jnp.split(p, [s0, s1, s2, s3, s4, s5], axis=-1)
    aq = to_heads(aq, ATTN_HEADS)
    ak = to_heads(ak, ATTN_KV_HEADS)
    av = to_heads(av, ATTN_KV_HEADS)
    rq = to_heads(rq, RET_HEADS)
    rk = to_heads(rk, RET_HEADS) * (RET_QK_DIM ** -0.5)
    rv = to_heads(rv, RET_HEADS)
    if rope is not None:
        cos, sin = rope
        aq = apply_rope(aq, cos, sin)
        ak = apply_rope(ak, cos, sin)
        rq = apply_rope(rq, cos, sin)
        rk = apply_rope(rk, cos, sin)
    return aq, ak, av, rq, rk, rv, rg


def window_attention(q, k, v, kc, vc, sink):
    b, hq, l, d = q.shape
    nb = l // ATTN_BLOCK
    nk = 3 * ATTN_BLOCK
    lc = kc.shape[2]
    qb = (q * (d ** -0.5)).reshape(b, ATTN_KV_HEADS, ATTN_GROUP, nb, ATTN_BLOCK, d)
    pad = ((0, 0), (0, 0), (ATTN_BLOCK, ATTN_BLOCK), (0, 0))
    kp = jnp.pad(k, pad).reshape(b, ATTN_KV_HEADS, nb + 2, ATTN_BLOCK, d)
    vp = jnp.pad(v, pad).reshape(b, ATTN_KV_HEADS, nb + 2, ATTN_BLOCK, d)
    kb = jnp.concatenate([kp[:, :, 0:nb], kp[:, :, 1:nb + 1], kp[:, :, 2:nb + 2]], axis=3)
    vb = jnp.concatenate([vp[:, :, 0:nb], vp[:, :, 1:nb + 1], vp[:, :, 2:nb + 2]], axis=3)
    qi = jnp.arange(nb)[:, None, None] * ATTN_BLOCK + jnp.arange(ATTN_BLOCK)[None, :, None]
    kj = (jnp.arange(nb)[:, None, None] - 1) * ATTN_BLOCK + jnp.arange(nk)[None, None, :]
    valid = (jnp.abs(kj - qi) <= WINDOW) & (kj >= 0) & (kj < l)
    s_loc = jnp.einsum('bhgnqd,bhnkd->bhgnqk', qb, kb).astype(F32)
    s_loc = jnp.where(valid, s_loc, -jnp.inf)
    s_ctx = jnp.einsum('bhgnqd,bhkd->bhgnqk', qb, kc).astype(F32)
    s_sink = jnp.broadcast_to(sink.astype(F32).reshape(ATTN_KV_HEADS, ATTN_GROUP, 1, 1, 1), s_loc.shape[:-1] + (1,))
    p = jax.nn.softmax(jnp.concatenate([s_loc, s_ctx, s_sink], axis=-1), axis=-1).astype(v.dtype)
    out = (jnp.einsum('bhgnqk,bhnkd->bhgnqd', p[..., :nk], vb)
           + jnp.einsum('bhgnqk,bhkd->bhgnqd', p[..., nk:nk + lc], vc))
    return out.reshape(b, hq, l, d)


def context_attention(q, k, v, sink):
    b, hq, lc, d = q.shape
    qg = (q * (d ** -0.5)).reshape(b, ATTN_KV_HEADS, ATTN_GROUP, lc, d)
    s = jnp.einsum('bhgqd,bhkd->bhgqk', qg, k).astype(F32)
    s_sink = jnp.broadcast_to(sink.astype(F32).reshape(ATTN_KV_HEADS, ATTN_GROUP, 1, 1), s.shape[:-1] + (1,))
    p = jax.nn.softmax(jnp.concatenate([s, s_sink], axis=-1), axis=-1).astype(v.dtype)
    out = jnp.einsum('bhgqk,bhkd->bhgqd', p[..., :lc], v)
    return out.reshape(b, hq, lc, d)


def retention_scan(q, k, v, log_gamma, state0, include_diag):
    b, h, l, dk = q.shape
    dv = v.shape[-1]
    n = l // RET_CHUNK
    dt = v.dtype
    lg = log_gamma.astype(F32)[:, None]
    pos = jnp.arange(RET_CHUNK, dtype=F32)
    rel = pos[:, None] - pos[None, :]
    mask = (rel >= 0) if include_diag else (rel > 0)
    decay_intra = jnp.where(mask, jnp.exp(lg[:, :, None] * jnp.maximum(rel, 0.0)), 0.0)
    q_decay = jnp.exp(lg * (pos + 1.0))
    k_decay = jnp.exp(lg * (RET_CHUNK - 1.0 - pos))
    chunk_decay = jnp.exp(lg[:, 0] * RET_CHUNK)
    qc = q.reshape(b, h, n, RET_CHUNK, dk)
    kc = k.reshape(b, h, n, RET_CHUNK, dk)
    vc = v.reshape(b, h, n, RET_CHUNK, dv)
    scores = jnp.einsum('bhncd,bhnsd->bhncs', qc, kc) * decay_intra[:, None].astype(dt)
    intra = jnp.einsum('bhncs,bhnse->bhnce', scores, vc)
    kv = jnp.einsum('bhnsd,bhnse->nbhde', kc * k_decay[:, None, :, None].astype(dt), vc)
    cd = chunk_decay[:, None, None].astype(state0.dtype)

    def step(s, kv_c):
        return s * cd + kv_c, s

    s_final, s_before = lax.scan(step, state0, kv)
    cross = jnp.einsum('bhncd,nbhde->bhnce', qc * q_decay[:, None, :, None].astype(dt), s_before)
    return (intra + cross).reshape(b, h, l, dv), s_final


def retention_output(y, gate, gn_g):
    b, h, l, dv = y.shape
    yf = y.astype(F32)
    mu = jnp.mean(yf, axis=-1, keepdims=True)
    var = jnp.mean(jnp.square(yf - mu), axis=-1, keepdims=True)
    yn = (yf - mu) * lax.rsqrt(var + EPS) * gn_g.astype(F32)[:, None, :]
    yn = yn.astype(y.dtype).transpose(0, 2, 1, 3).reshape(b, l, h * dv)
    return jax.nn.silu(gate) * yn


def merge(attn_o, ret_o, w_out):
    b, h, l, d = attn_o.shape
    a = attn_o.transpose(0, 2, 1, 3).reshape(b, l, h * d)
    return jnp.concatenate([a, ret_o], axis=-1) @ w_out


def sqrelu_mlp(h, w_up, w_down):
    return jnp.square(jax.nn.relu(h @ w_up)) @ w_down


def setup_inputs(seed: int = 0) -> dict:
    key = jax.random.key(seed)
    ks = jax.random.split(key, 14)
    nrm = jax.random.normal
    x = nrm(ks[0], (BATCH, SEQ, D_MODEL), F32)
    c = nrm(ks[1], (BATCH, D_MODEL), F32)
    ctx = nrm(ks[2], (BATCH, CTX_LEN, D_MODEL), F32)
    c_ctx = nrm(ks[3], (D_MODEL,), F32)
    w_ada = nrm(ks[4], (DEPTH, D_MODEL, 6 * D_MODEL), F32) * (0.5 * D_MODEL ** -0.5)
    b_ada = 0.02 * nrm(ks[5], (DEPTH, 6 * D_MODEL), F32)
    norm_g = 1.0 + 0.02 * nrm(ks[6], (DEPTH, 4, D_MODEL), F32)
    w_in = nrm(ks[7], (DEPTH, D_MODEL, IN_WIDTH), F32) * (D_MODEL ** -0.5)
    attn_sink = 0.5 * nrm(ks[8], (DEPTH, ATTN_HEADS), F32)
    base = jnp.log1p(-jnp.exp2(-5.0 - jnp.arange(RET_HEADS, dtype=F32)))
    ret_log_decay = base * (1.0 + 0.1 * nrm(ks[9], (DEPTH, 2, RET_HEADS), F32))
    ret_gn_g = 1.0 + 0.02 * nrm(ks[10], (DEPTH, RET_HEADS, RET_V_DIM), F32)
    w_out = nrm(ks[11], (DEPTH, MIX_WIDTH, D_MODEL), F32) * (MIX_WIDTH ** -0.5)
    w_up = nrm(ks[12], (DEPTH, D_MODEL, D_FF), F32) * (D_MODEL ** -0.5)
    w_down = nrm(ks[13], (DEPTH, D_FF, D_MODEL), F32) * (D_FF ** -0.5)
    return {'x': x, 'c': c, 'ctx': ctx, 'c_ctx': c_ctx, 'w_ada': w_ada, 'b_ada': b_ada, 'norm_g': norm_g,
            'w_in': w_in, 'attn_sink': attn_sink, 'ret_log_decay': ret_log_decay, 'ret_gn_g': ret_gn_g,
            'w_out': w_out, 'w_up': w_up, 'w_down': w_down}


def reference(x, c, ctx, c_ctx, w_ada, b_ada, norm_g, w_in, attn_sink, ret_log_decay, ret_gn_g, w_out, w_up, w_down):
    b, l, _ = x.shape
    ROWS = l // GRID_W
    rope = rope_2d(ROWS, l)
    xc = ctx
    for li in range(DEPTH):
        last = li == DEPTH - 1
        g_pre_mix, g_post_mix, g_pre_mlp, g_post_mlp = norm_g[li]
        sh1, sc1, gt1, sh2, sc2, gt2 = [m[:, None, :] for m in jnp.split(jax.nn.silu(c) @ w_ada[li] + b_ada[li], 6, axis=-1)]
        csh1, csc1, cgt1, csh2, csc2, cgt2 = jnp.split(jax.nn.silu(c_ctx) @ w_ada[li] + b_ada[li], 6, axis=-1)
        lg_f = ret_log_decay[li, 0]
        lg_b = ret_log_decay[li, 1]

        h = modulate(rmsnorm(x, g_pre_mix), sh1, sc1)
        hc = modulate(rmsnorm(xc, g_pre_mix), csh1, csc1)
        aq, ak, av, rq, rk, rv, rg = project(h, w_in[li], rope)
        caq, cak, cav, crq, crk, crv, crg = project(hc, w_in[li], None)

        zeros = jnp.zeros((b, RET_HEADS, RET_QK_DIM, RET_V_DIM), rv.dtype)
        cr_f, s_f = retention_scan(crq, crk, crv, lg_f, zeros, True)
        cr_b, s_b = retention_scan(flip(crq), flip(crk), flip(crv), lg_b, zeros, False)
        r_f, _ = retention_scan(rq, rk, rv, lg_f, s_f, True)
        r_b, _ = retention_scan(flip(rq), flip(rk), flip(rv), lg_b, s_b, False)
        ret = retention_output(r_f + flip(r_b), rg, ret_gn_g[li])
        att = window_attention(aq, ak, av, cak, cav, attn_sink[li])
        y = merge(att, ret, w_out[li])
        x = x + gt1 * rmsnorm(y, g_post_mix)
        if not last:
            catt = context_attention(caq, cak, cav, attn_sink[li])
            cret = retention_output(cr_f + flip(cr_b), crg, ret_gn_g[li])
            xc = xc + cgt1 * rmsnorm(merge(catt, cret, w_out[li]), g_post_mix)

        h = modulate(rmsnorm(x, g_pre_mlp), sh2, sc2)
        x = x + gt2 * rmsnorm(sqrelu_mlp(h, w_up[li], w_down[li]), g_post_mlp)
        if not last:
            hc = modulate(rmsnorm(xc, g_pre_mlp), csh2, csc2)
            xc = xc + cgt2 * rmsnorm(sqrelu_mlp(hc, w_up[li], w_down[li]), g_post_mlp)
    return x
```

```python
import functools

import jax
import jax.numpy as jnp
from jax import lax
from jax.experimental import pallas as pl
from jax.experimental.pallas import tpu as pltpu

F32 = jnp.float32
BF16 = jnp.bfloat16

HEAD_DIM = 128
GRID_W = 64
WINDOW = 128
ATTN_BLOCK = 128
ATTN_GROUP = 4
RET_CHUNK = 128
RET_V_DIM = 2 * HEAD_DIM
ROPE_BASE = 10000.0
EPS = 1e-6
MASKED = -1e30

LANES = 128
BF16_ROWS = 16
MOD_ROWS = 16
VMEM_LIMIT = 56 * 1024 * 1024


def _params(*sem):
    return pltpu.CompilerParams(dimension_semantics=sem, vmem_limit_bytes=VMEM_LIMIT)


def _adaln_body(c_ref, w_ref, b_ref, o_ref):
    c = c_ref[...]
    a = (c * jax.nn.sigmoid(c)).astype(BF16)
    o_ref[0] = jnp.dot(a, w_ref[0].astype(BF16), preferred_element_type=F32) + b_ref[0]


def _adaln(c_all, w_ada, b_ada, tn=512):
    depth, d, n = w_ada.shape
    rows = c_all.shape[0]
    return pl.pallas_call(
        _adaln_body,
        grid=(depth, n // tn),
        in_specs=[pl.BlockSpec((rows, d), lambda l, j: (0, 0)),
                  pl.BlockSpec((1, d, tn), lambda l, j: (l, 0, j)),
                  pl.BlockSpec((1, 1, tn), lambda l, j: (l, 0, j))],
        out_specs=pl.BlockSpec((1, rows, tn), lambda l, j: (l, 0, j)),
        out_shape=jax.ShapeDtypeStruct((depth, rows, n), F32),
        compiler_params=_params("parallel", "parallel"),
        name="adaln",
    )(c_all, w_ada, b_ada.reshape(depth, 1, n))


def _prenorm_body(x_ref, g_ref, sh_ref, sc_ref, o_ref):
    x = x_ref[...]
    r = lax.rsqrt(jnp.mean(x * x, axis=-1, keepdims=True) + EPS)
    o_ref[...] = (((x * r) * g_ref[...]) * (1.0 + sc_ref[0]) + sh_ref[0]).astype(BF16)


def _prenorm(x, g, sh, sc, group_of_tile, tm=256):
    m, d = x.shape
    mod_spec = pl.BlockSpec((1, 1, d), lambda i: (group_of_tile(i, tm), 0, 0))
    return pl.pallas_call(
        _prenorm_body,
        grid=(m // tm,),
        in_specs=[pl.BlockSpec((tm, d), lambda i: (i, 0)),
                  pl.BlockSpec((1, d), lambda i: (0, 0)),
                  mod_spec, mod_spec],
        out_specs=pl.BlockSpec((tm, d), lambda i: (i, 0)),
        out_shape=jax.ShapeDtypeStruct((m, d), BF16),
        compiler_params=_params("parallel"),
        name="prenorm",
    )(x, g.reshape(1, d), sh, sc)


def _inproj_body(h_ref, w_ref, *rest, kinds, rope, n_chunk, q_scale, rk_scale):
    if rope:
        cos_ref, sa_ref, sb_ref, o_ref = rest
    else:
        (o_ref,) = rest
    j = pl.program_id(1)
    acc = jnp.dot(h_ref[...], w_ref[...], preferred_element_type=F32)

    def rot(t):
        if not rope:
            return t
        return t * cos_ref[...] + (pltpu.roll(t, 96, 1) * sa_ref[...] + pltpu.roll(t, 32, 1) * sb_ref[...])

    fns = {
        "q": lambda t: rot(t) * q_scale,
        "rope": rot,
        "rk": lambda t: rot(t * rk_scale),
        "none": lambda t: t,
    }
    for kind in sorted(set(kinds)):
        tiles = [t for t, k in enumerate(kinds) if k == kind]
        cond = functools.reduce(jnp.logical_or, [j == t for t in tiles])

        @pl.when(cond)
        def _():
            for cc in range(n_chunk):
                o_ref[cc] = fns[kind](acc[:, cc * LANES:(cc + 1) * LANES]).astype(BF16)


def _inproj(h, w, layout, rope_tabs, seq_len, tm=1024, tn=512):
    m, d = h.shape
    n = w.shape[1]
    n_chunk = tn // LANES
    kinds = []
    for t in range(n // tn):
        ks = {layout.kind_of_chunk(c) for c in range(t * n_chunk, (t + 1) * n_chunk)}
        assert len(ks) == 1, "column tile must not straddle head groups"
        kinds.append(ks.pop())
    rope = rope_tabs is not None
    in_specs = [pl.BlockSpec((tm, d), lambda i, j: (i, 0)),
                pl.BlockSpec((d, tn), lambda i, j: (0, j))]
    args = [h, w]
    if rope:
        per_seq = seq_len // tm
        tab_spec = pl.BlockSpec((tm, LANES), lambda i, j: (i % per_seq, 0))
        in_specs += [tab_spec] * 3
        args += list(rope_tabs)
    return pl.pallas_call(
        functools.partial(_inproj_body, kinds=tuple(kinds), rope=rope, n_chunk=n_chunk,
                          q_scale=HEAD_DIM ** -0.5, rk_scale=HEAD_DIM ** -0.5),
        grid=(m // tm, n // tn),
        in_specs=in_specs,
        out_specs=pl.BlockSpec((n_chunk, tm, LANES), lambda i, j: (j, i, 0)),
        out_shape=jax.ShapeDtypeStruct((n // LANES, m, LANES), BF16),
        compiler_params=_params("parallel", "arbitrary"),
        name="inproj_rope" if rope else "inproj_ctx",
    )(*args)


def _attn_body(*refs, local, tq, n_qblocks):
    if local:
        q_ref, kp_ref, kc_ref, kn_ref, vp_ref, vc_ref, vn_ref, ck_ref, cv_ref, sink_ref, o_ref = refs
    else:
        q_ref, ck_ref, cv_ref, sink_ref, o_ref = refs
    g = q_ref.shape[0]
    rows = g * tq
    q = q_ref[...].reshape(rows, HEAD_DIM)
    if local:
        keys = jnp.concatenate([kp_ref[0], kc_ref[0], kn_ref[0], ck_ref[0]], axis=0)
        vals = jnp.concatenate([vp_ref[0], vc_ref[0], vn_ref[0], cv_ref[0]], axis=0)
    else:
        keys, vals = ck_ref[0], cv_ref[0]
    s = lax.dot_general(q, keys, (((1,), (1,)), ((), ())), preferred_element_type=F32)
    if local:
        n = pl.program_id(2)
        nk = s.shape[1]
        qq = lax.broadcasted_iota(jnp.int32, (rows, nk), 0) & (tq - 1)
        cc = lax.broadcasted_iota(jnp.int32, (rows, nk), 1)
        off = cc - qq
        lo = jnp.where(n == 0, ATTN_BLOCK, 0)
        hi = jnp.where(n == n_qblocks - 1, 2 * ATTN_BLOCK, 3 * ATTN_BLOCK)
        in_band = (off >= 0) & (off <= 2 * WINDOW) & (cc >= lo) & (cc < hi)
        valid = in_band | (cc >= 3 * ATTN_BLOCK)
        s = jnp.where(valid, s, MASKED)
    sink = sink_ref[0]
    m = jnp.maximum(jnp.max(s, axis=-1, keepdims=True), sink)
    p = jnp.exp(s - m)
    denom = jnp.sum(p, axis=-1, keepdims=True) + jnp.exp(sink - m)
    o = jnp.dot(p.astype(BF16), vals, preferred_element_type=F32) / denom
    for gi in range(g):
        o_ref[:, gi * HEAD_DIM:(gi + 1) * HEAD_DIM] = o[gi * tq:(gi + 1) * tq].astype(BF16)


def _attention(p_lat, p_ctx, sink_col, layout, batch, seq_len, ctx_len):
    g, tq = ATTN_GROUP, ATTN_BLOCK
    nq = seq_len // tq
    kvh = layout.kv_heads
    ak, av = layout.ak0, layout.av0
    blk = (1, tq, HEAD_DIM)
    prev = lambda b, h, n: b * nq + jnp.maximum(n - 1, 0)
    cur = lambda b, h, n: b * nq + n
    nxt = lambda b, h, n: b * nq + jnp.minimum(n + 1, nq - 1)
    in_specs = [pl.BlockSpec((g, tq, HEAD_DIM), lambda b, h, n: (h, cur(b, h, n), 0))]
    for base in (ak, av):
        for row in (prev, cur, nxt):
            in_specs.append(pl.BlockSpec(blk, functools.partial(
                lambda b, h, n, base, row: (base + h, row(b, h, n), 0), base=base, row=row)))
    in_specs += [pl.BlockSpec((1, ctx_len, HEAD_DIM), lambda b, h, n: (ak + h, b, 0)),
                 pl.BlockSpec((1, ctx_len, HEAD_DIM), lambda b, h, n: (av + h, b, 0)),
                 pl.BlockSpec((1, g * tq, 1), lambda b, h, n: (h, 0, 0))]
    return pl.pallas_call(
        functools.partial(_attn_body, local=True, tq=tq, n_qblocks=nq),
        grid=(batch, kvh, nq),
        in_specs=in_specs,
        out_specs=pl.BlockSpec((tq, g * HEAD_DIM), lambda b, h, n: (cur(b, h, n), h)),
        out_shape=jax.ShapeDtypeStruct((batch * seq_len, kvh * g * HEAD_DIM), BF16),
        compiler_params=_params("parallel", "parallel", "parallel"),
        name="window_attention",
    )(p_lat, *([p_lat] * 6), p_ctx, p_ctx, sink_col)


def _ctx_attention(p_ctx, sink_col, layout, batch, ctx_len):
    g = ATTN_GROUP
    kvh = layout.kv_heads
    ak, av = layout.ak0, layout.av0
    return pl.pallas_call(
        functools.partial(_attn_body, local=False, tq=ctx_len, n_qblocks=1),
        grid=(batch, kvh),
        in_specs=[pl.BlockSpec((g, ctx_len, HEAD_DIM), lambda b, h: (h, b, 0)),
                  pl.BlockSpec((1, ctx_len, HEAD_DIM), lambda b, h: (ak + h, b, 0)),
                  pl.BlockSpec((1, ctx_len, HEAD_DIM), lambda b, h: (av + h, b, 0)),
                  pl.BlockSpec((1, g * ctx_len, 1), lambda b, h: (h, 0, 0))],
        out_specs=pl.BlockSpec((ctx_len, g * HEAD_DIM), lambda b, h: (b, h)),
        out_shape=jax.ShapeDtypeStruct((batch * ctx_len, kvh * g * HEAD_DIM), BF16),
        compiler_params=_params("parallel", "parallel"),
        name="context_attention",
    )(p_ctx, p_ctx, p_ctx, sink_col)


def _ret_body(lg_ref, q_ref, k_ref, v_ref, g_ref, *rest, n_lat, n_ctx, ctx_out):
    if ctx_out:
        cq_ref, ck_ref, cv_ref, cg_ref, gn_ref, o_ref, co_ref, yf_ref = rest
    else:
        ck_ref, cv_ref, gn_ref, o_ref, yf_ref = rest
        cq_ref = cg_ref = co_ref = None
    c = RET_CHUNK
    h = pl.program_id(1)
    lgf = lg_ref[0, h]
    lgb = lg_ref[1, h]
    ii = lax.broadcasted_iota(jnp.int32, (c, c), 0)
    jj = lax.broadcasted_iota(jnp.int32, (c, c), 1)
    rel = (ii - jj).astype(F32)
    dm_f = jnp.where(rel >= 0, jnp.exp(lgf * jnp.maximum(rel, 0.0)), 0.0)
    dm_b = jnp.where(rel < 0, jnp.exp(lgb * jnp.maximum(-rel, 0.0)), 0.0)
    pos = lax.broadcasted_iota(jnp.int32, (c, HEAD_DIM), 0).astype(F32)
    qd_f = jnp.exp(lgf * (pos + 1.0))
    kd_f = jnp.exp(lgf * (c - 1.0 - pos))
    qd_b = jnp.exp(lgb * (c - pos))
    kd_b = jnp.exp(lgb * pos)
    cd_f = jnp.exp(jnp.full((1, RET_V_DIM), lgf * c, F32))
    cd_b = jnp.exp(jnp.full((1, RET_V_DIM), lgb * c, F32))
    gn = gn_ref[0]

    def rows(ci):
        return slice(ci * c, (ci + 1) * c)

    def wide(ref, ci):
        return jnp.concatenate([ref[0, rows(ci), :], ref[1, rows(ci), :]], axis=1)

    def step(q, k, v, state, dm, qd, kd, cd):
        kv = lax.dot_general((k.astype(F32) * kd).astype(BF16), v, (((0,), (0,)), ((), ())),
                             preferred_element_type=F32)
        out = None
        if q is not None:
            s = lax.dot_general(q, k, (((1,), (1,)), ((), ())), preferred_element_type=F32) * dm
            out = jnp.dot(s.astype(BF16), v, preferred_element_type=F32)
            if state is not None:
                out = out + jnp.dot((q.astype(F32) * qd).astype(BF16), state.astype(BF16),
                                    preferred_element_type=F32)
        return out, (kv if state is None else state * cd + kv)

    def finish(y, gate):
        mu = jnp.mean(y, axis=-1, keepdims=True)
        dlt = y - mu
        var = jnp.mean(dlt * dlt, axis=-1, keepdims=True)
        yn = dlt * lax.rsqrt(var + EPS) * gn
        gf = gate.astype(F32)
        return ((gf * jax.nn.sigmoid(gf)) * yn).astype(BF16)

    n_y_ctx = n_ctx if ctx_out else 0

    state = None
    for ci in range(n_ctx):
        q = cq_ref[0, rows(ci), :] if ctx_out else None
        out, state = step(q, ck_ref[0, rows(ci), :], wide(cv_ref, ci), state, dm_f, qd_f, kd_f, cd_f)
        if ctx_out:
            yf_ref[rows(ci), :] = out
    for ci in range(n_lat):
        out, state = step(q_ref[0, rows(ci), :], k_ref[0, rows(ci), :], wide(v_ref, ci), state,
                          dm_f, qd_f, kd_f, cd_f)
        yf_ref[rows(n_y_ctx + ci), :] = out

    state = None
    for ci in reversed(range(n_ctx)):
        q = cq_ref[0, rows(ci), :] if ctx_out else None
        out, state = step(q, ck_ref[0, rows(ci), :], wide(cv_ref, ci), state, dm_b, qd_b, kd_b, cd_b)
        if ctx_out:
            co_ref[rows(ci), :] = finish(yf_ref[rows(ci), :] + out, wide(cg_ref, ci))
    for ci in reversed(range(n_lat)):
        out, state = step(q_ref[0, rows(ci), :], k_ref[0, rows(ci), :], wide(v_ref, ci), state,
                          dm_b, qd_b, kd_b, cd_b)
        o_ref[rows(ci), :] = finish(yf_ref[rows(n_y_ctx + ci), :] + out, wide(g_ref, ci))


def _retention(p_lat, p_ctx, log_decay, gn_g, layout, batch, seq_len, ctx_len, ctx_out):
    nh = layout.ret_heads
    rq, rk, rv, rg = layout.rq0, layout.rk0, layout.rv0 // 2, layout.rg0 // 2
    narrow = lambda base, ln: pl.BlockSpec((1, ln, HEAD_DIM), lambda b, h, lg: (base + h, b, 0))
    wide = lambda base, ln: pl.BlockSpec((2, ln, HEAD_DIM), lambda b, h, lg: (base + h, b, 0))
    in_specs = [narrow(rq, seq_len), narrow(rk, seq_len), wide(rv, seq_len), wide(rg, seq_len)]
    args = [p_lat] * 4
    if ctx_out:
        in_specs += [narrow(rq, ctx_len), narrow(rk, ctx_len), wide(rv, ctx_len), wide(rg, ctx_len)]
        args += [p_ctx] * 4
    else:
        in_specs += [narrow(rk, ctx_len), wide(rv, ctx_len)]
        args += [p_ctx] * 2
    in_specs.append(pl.BlockSpec((1, 1, RET_V_DIM), lambda b, h, lg: (h, 0, 0)))
    args.append(gn_g.reshape(nh, 1, RET_V_DIM))
    out_specs = [pl.BlockSpec((seq_len, RET_V_DIM), lambda b, h, lg: (b, h))]
    out_shape = [jax.ShapeDtypeStruct((batch * seq_len, nh * RET_V_DIM), BF16)]
    y_rows = seq_len
    if ctx_out:
        out_specs.append(pl.BlockSpec((ctx_len, RET_V_DIM), lambda b, h, lg: (b, h)))
        out_shape.append(jax.ShapeDtypeStruct((batch * ctx_len, nh * RET_V_DIM), BF16))
        y_rows += ctx_len
    outs = pl.pallas_call(
        functools.partial(_ret_body, n_lat=seq_len // RET_CHUNK, n_ctx=ctx_len // RET_CHUNK, ctx_out=ctx_out),
        grid_spec=pltpu.PrefetchScalarGridSpec(
            num_scalar_prefetch=1,
            grid=(batch, nh),
            in_specs=in_specs,
            out_specs=out_specs,
            scratch_shapes=[pltpu.VMEM((y_rows, RET_V_DIM), F32)]),
        out_shape=out_shape,
        compiler_params=_params("parallel", "parallel"),
        name="retention" if ctx_out else "retention_last",
    )(log_decay, *args)
    return (outs[0], outs[1]) if ctx_out else (outs[0], None)


def _residual_epilogue(xo_ref, x_ref, gt_ref, gpost_ref, nxt, tm):
    gt = gt_ref[0]
    gpost = gpost_ref[...]
    if nxt is not None:
        gpre_ref, sh_ref, sc_ref, hn_ref = nxt
        gpre = gpre_ref[...]
        sc1 = 1.0 + sc_ref[0]
        sh = sh_ref[0]

    def body(rb, carry):
        rws = pl.ds(pl.multiple_of(rb * BF16_ROWS, BF16_ROWS), BF16_ROWS)
        y = xo_ref[rws, :]
        r = lax.rsqrt(jnp.mean(y * y, axis=-1, keepdims=True) + EPS)
        xn = x_ref[rws, :] + gt * ((y * r) * gpost)
        xo_ref[rws, :] = xn
        if nxt is not None:
            r2 = lax.rsqrt(jnp.mean(xn * xn, axis=-1, keepdims=True) + EPS)
            hn_ref[rws, :] = (((xn * r2) * gpre) * sc1 + sh).astype(BF16)
        return carry

    lax.fori_loop(0, tm // BF16_ROWS, body, 0)


def _outproj_body(a_ref, r_ref, wa_ref, wb_ref, x_ref, gt_ref, gpost_ref, gpre_ref, sh_ref, sc_ref,
                  xo_ref, hn_ref, *, nj, tn, tm):
    j = pl.program_id(1)
    y = (jnp.dot(a_ref[...], wa_ref[...], preferred_element_type=F32)
         + jnp.dot(r_ref[...], wb_ref[...], preferred_element_type=F32))
    for jj in range(nj):
        @pl.when(j == jj)
        def _():
            xo_ref[:, jj * tn:(jj + 1) * tn] = y

    @pl.when(j == nj - 1)
    def _():
        _residual_epilogue(xo_ref, x_ref, gt_ref, gpost_ref, (gpre_ref, sh_ref, sc_ref, hn_ref), tm)


def _resident(shape, index_map):
    return pl.BlockSpec(shape, index_map, pipeline_mode=pl.Buffered(1))


def _outproj(att, ret, w_out, x, gt, g_post, g_pre, sh, sc, group_of_tile, tm=512, tn=512):
    m, d = x.shape
    ka = att.shape[1]
    assert ret.shape[1] == ka and w_out.shape[0] == 2 * ka
    nj = d // tn
    mod_spec = pl.BlockSpec((1, 1, d), lambda i, j: (group_of_tile(i, tm), 0, 0))
    vec_spec = pl.BlockSpec((1, d), lambda i, j: (0, 0))
    row_tile = lambda i, j: (i, 0)
    return pl.pallas_call(
        functools.partial(_outproj_body, nj=nj, tn=tn, tm=tm),
        grid=(m // tm, nj),
        in_specs=[pl.BlockSpec((tm, ka), row_tile), pl.BlockSpec((tm, ka), row_tile),
                  pl.BlockSpec((ka, tn), lambda i, j: (0, j)), pl.BlockSpec((ka, tn), lambda i, j: (1, j)),
                  _resident((tm, d), row_tile), mod_spec, vec_spec, vec_spec, mod_spec, mod_spec],
        out_specs=[pl.BlockSpec((tm, d), row_tile), pl.BlockSpec((tm, d), row_tile)],
        out_shape=[jax.ShapeDtypeStruct((m, d), F32), jax.ShapeDtypeStruct((m, d), BF16)],
        compiler_params=_params("parallel", "arbitrary"),
        name="outproj",
    )(att, ret, w_out, w_out, x, gt, g_post.reshape(1, d), g_pre.reshape(1, d), sh, sc)


def _mlp_body(h_ref, wu_ref, wd_ref, x_ref, gt_ref, gpost_ref, *rest, nf, nt, tm, has_next):
    if has_next:
        gpre_ref, sh_ref, sc_ref, xo_ref, hn_ref = rest
        nxt = (gpre_ref, sh_ref, sc_ref, hn_ref)
    else:
        (xo_ref,) = rest
        nxt = None
    f = pl.program_id(1)

    @pl.when(f == 0)
    def _():
        xo_ref[...] = jnp.zeros_like(xo_ref)

    u = jnp.dot(h_ref[...], wu_ref[...], preferred_element_type=F32)
    a = jnp.maximum(u, 0.0)
    a = (a * a).astype(BF16)
    for nn in range(xo_ref.shape[1] // nt):
        cols = slice(nn * nt, (nn + 1) * nt)
        xo_ref[:, cols] += jnp.dot(a, wd_ref[:, cols], preferred_element_type=F32)

    @pl.when(f == nf - 1)
    def _():
        _residual_epilogue(xo_ref, x_ref, gt_ref, gpost_ref, nxt, tm)


def _mlp(h, w_up, w_down, x, gt, g_post, nxt, group_of_tile, tm=512, tf=256, nt=512):
    m, d = x.shape
    ff = w_up.shape[1]
    nf = ff // tf
    mod_spec = pl.BlockSpec((1, 1, d), lambda i, f: (group_of_tile(i, tm), 0, 0))
    vec_spec = pl.BlockSpec((1, d), lambda i, f: (0, 0))
    row_tile = lambda i, f: (i, 0)
    in_specs = [_resident((tm, d), row_tile),
                pl.BlockSpec((d, tf), lambda i, f: (0, f)),
                pl.BlockSpec((tf, d), lambda i, f: (f, 0)),
                _resident((tm, d), row_tile), mod_spec, vec_spec]
    args = [h, w_up, w_down, x, gt, g_post.reshape(1, d)]
    out_specs = [pl.BlockSpec((tm, d), row_tile)]
    out_shape = [jax.ShapeDtypeStruct((m, d), F32)]
    if nxt is not None:
        g_pre, sh, sc = nxt
        in_specs += [vec_spec, mod_spec, mod_spec]
        args += [g_pre.reshape(1, d), sh, sc]
        out_specs.append(pl.BlockSpec((tm, d), row_tile))
        out_shape.append(jax.ShapeDtypeStruct((m, d), BF16))
    outs = pl.pallas_call(
        functools.partial(_mlp_body, nf=nf, nt=nt, tm=tm, has_next=nxt is not None),
        grid=(m // tm, nf),
        in_specs=in_specs,
        out_specs=out_specs,
        out_shape=out_shape,
        compiler_params=_params("parallel", "arbitrary"),
        name="mlp" if nxt is not None else "mlp_last",
    )(*args)
    return (outs[0], outs[1]) if nxt is not None else (outs[0], None)


class _Layout:
    def __init__(self, d_model):
        self.attn_heads = d_model // (2 * HEAD_DIM)
        self.kv_heads = self.attn_heads // ATTN_GROUP
        self.ret_heads = d_model // (2 * RET_V_DIM)
        vchunks = RET_V_DIM // LANES
        self.aq0 = 0
        self.ak0 = self.aq0 + self.attn_heads
        self.av0 = self.ak0 + self.kv_heads
        self.rq0 = self.av0 + self.kv_heads
        self.rk0 = self.rq0 + self.ret_heads
        self.rv0 = self.rk0 + self.ret_heads
        self.rg0 = self.rv0 + vchunks * self.ret_heads
        self.n_chunks = self.rg0 + vchunks * self.ret_heads

    def kind_of_chunk(self, c):
        if c < self.ak0:
            return "q"
        if c < self.av0:
            return "rope"
        if c < self.rq0:
            return "none"
        if c < self.rk0:
            return "rope"
        if c < self.rv0:
            return "rk"
        return "none"


def _rope_tables(seq_len):
    pairs = HEAD_DIM // 4
    t = jnp.arange(seq_len)
    r = (t // GRID_W).astype(F32)
    cl = (t % GRID_W).astype(F32)
    inv = ROPE_BASE ** (-jnp.arange(pairs, dtype=F32) / pairs)
    ar = r[:, None] * inv
    ac = cl[:, None] * inv
    ang = jnp.concatenate([ar, ar, ac, ac], axis=-1)
    cos, sin = jnp.cos(ang), jnp.sin(ang)
    first = (jnp.arange(HEAD_DIM) % (2 * pairs)) < pairs
    return cos, jnp.where(first, -sin, 0.0), jnp.where(first, 0.0, sin)


def kernel(x, c, ctx, c_ctx, w_ada, b_ada, norm_g, w_in, attn_sink, ret_log_decay, ret_gn_g, w_out, w_up, w_down):
    batch, seq_len, d = x.shape
    ctx_len = ctx.shape[1]
    depth = w_in.shape[0]
    layout = _Layout(d)
    assert layout.n_chunks * LANES == w_in.shape[2]
    assert batch + 1 <= MOD_ROWS

    xf = x.reshape(batch * seq_len, d)
    cf = ctx.reshape(batch * ctx_len, d)

    c_all = jnp.concatenate([c, c_ctx[None, :], jnp.zeros((MOD_ROWS - batch - 1, d), F32)], axis=0)
    mod = _adaln(c_all, w_ada, b_ada)
    mod = mod.reshape(depth, MOD_ROWS, 6, 1, d).transpose(0, 2, 1, 3, 4)
    lat_group = lambda i, tm: (i * tm) // seq_len
    ctx_group = lambda i, tm: batch

    rope_tabs = _rope_tables(seq_len)
    g = ATTN_GROUP

    h = _prenorm(xf, norm_g[0, 0], mod[0, 0], mod[0, 1], lat_group)
    hc = _prenorm(cf, norm_g[0, 0], mod[0, 0], mod[0, 1], ctx_group)
    for li in range(depth):
        last = li == depth - 1
        sh1, sc1, gt1, sh2, sc2, gt2 = [mod[li, k] for k in range(6)]
        g_pre_mix, g_post_mix, g_pre_mlp, g_post_mlp = [norm_g[li, k] for k in range(4)]
        wi = w_in[li].astype(BF16)
        wo = w_out[li].astype(BF16)
        wu = w_up[li].astype(BF16)
        wd = w_down[li].astype(BF16)
        sink = attn_sink[li].reshape(layout.kv_heads, g)

        p_lat = _inproj(h, wi, layout, rope_tabs, seq_len)
        p_ctx = _inproj(hc, wi, layout, None, ctx_len)
        sink_lat = jnp.repeat(sink, ATTN_BLOCK, axis=1)[:, :, None]
        att = _attention(p_lat, p_ctx, sink_lat, layout, batch, seq_len, ctx_len)
        ret, cret = _retention(p_lat, p_ctx, ret_log_decay[li], ret_gn_g[li], layout,
                               batch, seq_len, ctx_len, ctx_out=not last)
        xf, h2 = _outproj(att, ret, wo, xf, gt1, g_post_mix, g_pre_mlp, sh2, sc2, lat_group)
        nxt = None if last else (norm_g[li + 1, 0], mod[li + 1, 0], mod[li + 1, 1])
        xf, h = _mlp(h2, wu, wd, xf, gt2, g_post_mlp, nxt, lat_group)
        if not last:
            sink_ctx = jnp.repeat(sink, ctx_len, axis=1)[:, :, None]
            catt = _ctx_attention(p_ctx, sink_ctx, layout, batch, ctx_len)
            cf, hc2 = _outproj(catt, cret, wo, cf, gt1, g_post_mix, g_pre_mlp, sh2, sc2, ctx_group)
            cf, hc = _mlp(hc2, wu, wd, cf, gt2, g_post_mlp, nxt, ctx_group)
    return xf.reshape(batch, seq_len, d)
```

```python
import functools

import jax
import jax.numpy as jnp
from jax import lax
from jax.experimental import pallas as pl
from jax.experimental.pallas import tpu as pltpu

F32 = jnp.float32
BF16 = jnp.bfloat16

HEAD_DIM = 128
GRID_W = 64
WINDOW = 128
ATTN_BLOCK = 128
ATTN_GROUP = 4
RET_CHUNK = 128
RET_V_DIM = 2 * HEAD_DIM
ROPE_BASE = 10000.0
EPS = 1e-6
MASKED = -1e30

LANES = 128
SUBLANES = 8
BF16_ROWS = 16
MOD_ROWS = 16
VMEM_LIMIT = 56 * 1024 * 1024


def _rep8(v):
    return jnp.broadcast_to(v.reshape(1, -1), (SUBLANES, v.shape[-1]))


def _params(*sem):
    return pltpu.CompilerParams(dimension_semantics=sem, vmem_limit_bytes=VMEM_LIMIT)


def _adaln_body(c_ref, w_ref, b_ref, o_ref):
    c = c_ref[...]
    a = (c * jax.nn.sigmoid(c)).astype(BF16)
    o_ref[0] = jnp.dot(a, w_ref[0].astype(BF16), preferred_element_type=F32) + b_ref[0]


def _adaln(c_all, w_ada, b_ada, tn=512):
    depth, d, n = w_ada.shape
    rows = c_all.shape[0]
    return pl.pallas_call(
        _adaln_body,
        grid=(depth, n // tn),
        in_specs=[pl.BlockSpec((rows, d), lambda l, j: (0, 0)),
                  pl.BlockSpec((1, d, tn), lambda l, j: (l, 0, j)),
                  pl.BlockSpec((1, 1, tn), lambda l, j: (l, 0, j))],
        out_specs=pl.BlockSpec((1, rows, tn), lambda l, j: (l, 0, j)),
        out_shape=jax.ShapeDtypeStruct((depth, rows, n), F32),
        compiler_params=_params("parallel", "parallel"),
        name="adaln",
    )(c_all, w_ada, b_ada.reshape(depth, 1, n))


def _prenorm_body(x_ref, g_ref, sh_ref, sc_ref, o_ref):
    tm, d = x_ref.shape
    x = x_ref[...].reshape(tm // SUBLANES, SUBLANES, d)
    r = lax.rsqrt(jnp.mean(x * x, axis=-1, keepdims=True) + EPS)
    h = ((x * r) * g_ref[...][None]) * (1.0 + sc_ref[...]) + sh_ref[...]
    o_ref[...] = h.reshape(tm, d).astype(BF16)


def _prenorm(x, g, sh, sc, group_of_tile, tm=256):
    m, d = x.shape
    mod_spec = pl.BlockSpec((1, SUBLANES, d), lambda i: (group_of_tile(i, tm), 0, 0))
    return pl.pallas_call(
        _prenorm_body,
        grid=(m // tm,),
        in_specs=[pl.BlockSpec((tm, d), lambda i: (i, 0)),
                  pl.BlockSpec((SUBLANES, d), lambda i: (0, 0)),
                  mod_spec, mod_spec],
        out_specs=pl.BlockSpec((tm, d), lambda i: (i, 0)),
        out_shape=jax.ShapeDtypeStruct((m, d), BF16),
        compiler_params=_params("parallel"),
        name="prenorm",
    )(x, _rep8(g), sh, sc)


def _inproj_body(h_ref, w_ref, *rest, kinds, rope, n_chunk, q_scale, rk_scale):
    if rope:
        cos_ref, sa_ref, sb_ref, o_ref = rest
    else:
        (o_ref,) = rest
    j = pl.program_id(1)
    acc = jnp.dot(h_ref[...], w_ref[0], preferred_element_type=F32)

    def rot(t):
        if not rope:
            return t
        return t * cos_ref[...] + (pltpu.roll(t, 96, 1) * sa_ref[...] + pltpu.roll(t, 32, 1) * sb_ref[...])

    fns = {
        "q": lambda t: rot(t) * q_scale,
        "rope": rot,
        "rk": lambda t: rot(t * rk_scale),
        "none": lambda t: t,
    }
    for kind in sorted(set(kinds)):
        tiles = [t for t, k in enumerate(kinds) if k == kind]
        cond = functools.reduce(jnp.logical_or, [j == t for t in tiles])

        @pl.when(cond)
        def _():
            for cc in range(n_chunk):
                o_ref[cc] = fns[kind](acc[:, cc * LANES:(cc + 1) * LANES]).astype(BF16)


def _inproj(h, w, li, layout, rope_tabs, seq_len, tm=1024, tn=512):
    m, d = h.shape
    n = w.shape[2]
    n_chunk = tn // LANES
    kinds = []
    for t in range(n // tn):
        ks = {layout.kind_of_chunk(c) for c in range(t * n_chunk, (t + 1) * n_chunk)}
        assert len(ks) == 1, "column tile must not straddle head groups"
        kinds.append(ks.pop())
    rope = rope_tabs is not None
    in_specs = [pl.BlockSpec((tm, d), lambda i, j: (i, 0)),
                pl.BlockSpec((1, d, tn), lambda i, j: (li, 0, j))]
    args = [h, w]
    if rope:
        per_seq = seq_len // tm
        tab_spec = pl.BlockSpec((tm, LANES), lambda i, j: (i % per_seq, 0))
        in_specs += [tab_spec] * 3
        args += list(rope_tabs)
    return pl.pallas_call(
        functools.partial(_inproj_body, kinds=tuple(kinds), rope=rope, n_chunk=n_chunk,
                          q_scale=HEAD_DIM ** -0.5, rk_scale=HEAD_DIM ** -0.5),
        grid=(m // tm, n // tn),
        in_specs=in_specs,
        out_specs=pl.BlockSpec((n_chunk, tm, LANES), lambda i, j: (j, i, 0)),
        out_shape=jax.ShapeDtypeStruct((n // LANES, m, LANES), BF16),
        compiler_params=_params("parallel", "arbitrary"),
        name="inproj_rope" if rope else "inproj_ctx",
    )(*args)


def _attn_body(*refs, local, tq, n_qblocks):
    if local:
        q_ref, kp_ref, kc_ref, kn_ref, vp_ref, vc_ref, vn_ref, ck_ref, cv_ref, sink_ref, o_ref = refs
    else:
        q_ref, ck_ref, cv_ref, sink_ref, o_ref = refs
    g = q_ref.shape[0]
    rows = g * tq
    q = q_ref[...].reshape(rows, HEAD_DIM)
    if local:
        keys = jnp.concatenate([kp_ref[0], kc_ref[0], kn_ref[0], ck_ref[0]], axis=0)
        vals = jnp.concatenate([vp_ref[0], vc_ref[0], vn_ref[0], cv_ref[0]], axis=0)
    else:
        keys, vals = ck_ref[0], cv_ref[0]
    s = lax.dot_general(q, keys, (((1,), (1,)), ((), ())), preferred_element_type=F32)
    if local:
        n = pl.program_id(2)
        nk = s.shape[1]
        qq = lax.broadcasted_iota(jnp.int32, (rows, nk), 0) & (tq - 1)
        cc = lax.broadcasted_iota(jnp.int32, (rows, nk), 1)
        off = cc - qq
        lo = jnp.where(n == 0, ATTN_BLOCK, 0)
        hi = jnp.where(n == n_qblocks - 1, 2 * ATTN_BLOCK, 3 * ATTN_BLOCK)
        in_band = (off >= 0) & (off <= 2 * WINDOW) & (cc >= lo) & (cc < hi)
        valid = in_band | (cc >= 3 * ATTN_BLOCK)
        s = jnp.where(valid, s, MASKED)
    sink = sink_ref[0]
    m = jnp.maximum(jnp.max(s, axis=-1, keepdims=True), sink)
    p = jnp.exp(s - m)
    denom = jnp.sum(p, axis=-1, keepdims=True) + jnp.exp(sink - m)
    o = jnp.dot(p.astype(BF16), vals, preferred_element_type=F32) / denom
    for gi in range(g):
        o_ref[:, gi * HEAD_DIM:(gi + 1) * HEAD_DIM] = o[gi * tq:(gi + 1) * tq].astype(BF16)


def _attention(p_lat, p_ctx, sink_col, layout, batch, seq_len, ctx_len):
    g, tq = ATTN_GROUP, ATTN_BLOCK
    nq = seq_len // tq
    kvh = layout.kv_heads
    ak, av = layout.ak0, layout.av0
    blk = (1, tq, HEAD_DIM)
    prev = lambda b, h, n: b * nq + jnp.maximum(n - 1, 0)
    cur = lambda b, h, n: b * nq + n
    nxt = lambda b, h, n: b * nq + jnp.minimum(n + 1, nq - 1)
    in_specs = [pl.BlockSpec((g, tq, HEAD_DIM), lambda b, h, n: (h, cur(b, h, n), 0))]
    for base in (ak, av):
        for row in (prev, cur, nxt):
            in_specs.append(pl.BlockSpec(blk, functools.partial(
                lambda b, h, n, base, row: (base + h, row(b, h, n), 0), base=base, row=row)))
    in_specs += [pl.BlockSpec((1, ctx_len, HEAD_DIM), lambda b, h, n: (ak + h, b, 0)),
                 pl.BlockSpec((1, ctx_len, HEAD_DIM), lambda b, h, n: (av + h, b, 0)),
                 pl.BlockSpec((1, g * tq, 1), lambda b, h, n: (h, 0, 0))]
    return pl.pallas_call(
        functools.partial(_attn_body, local=True, tq=tq, n_qblocks=nq),
        grid=(batch, kvh, nq),
        in_specs=in_specs,
        out_specs=pl.BlockSpec((tq, g * HEAD_DIM), lambda b, h, n: (cur(b, h, n), h)),
        out_shape=jax.ShapeDtypeStruct((batch * seq_len, kvh * g * HEAD_DIM), BF16),
        compiler_params=_params("parallel", "parallel", "parallel"),
        name="window_attention",
    )(p_lat, *([p_lat] * 6), p_ctx, p_ctx, sink_col)


def _ctx_attention(p_ctx, sink_col, layout, batch, ctx_len):
    g = ATTN_GROUP
    kvh = layout.kv_heads
    ak, av = layout.ak0, layout.av0
    return pl.pallas_call(
        functools.partial(_attn_body, local=False, tq=ctx_len, n_qblocks=1),
        grid=(batch, kvh),
        in_specs=[pl.BlockSpec((g, ctx_len, HEAD_DIM), lambda b, h: (h, b, 0)),
                  pl.BlockSpec((1, ctx_len, HEAD_DIM), lambda b, h: (ak + h, b, 0)),
                  pl.BlockSpec((1, ctx_len, HEAD_DIM), lambda b, h: (av + h, b, 0)),
                  pl.BlockSpec((1, g * ctx_len, 1), lambda b, h: (h, 0, 0))],
        out_specs=pl.BlockSpec((ctx_len, g * HEAD_DIM), lambda b, h: (b, h)),
        out_shape=jax.ShapeDtypeStruct((batch * ctx_len, kvh * g * HEAD_DIM), BF16),
        compiler_params=_params("parallel", "parallel"),
        name="context_attention",
    )(p_ctx, p_ctx, p_ctx, sink_col)


def _ret_body(lg_ref, q_ref, k_ref, v_ref, g_ref, *rest, n_lat, n_ctx, ctx_out):
    if ctx_out:
        cq_ref, ck_ref, cv_ref, cg_ref, gn_ref, o_ref, co_ref, yf_ref = rest
    else:
        ck_ref, cv_ref, gn_ref, o_ref, yf_ref = rest
        cq_ref = cg_ref = co_ref = None
    c = RET_CHUNK
    h = pl.program_id(1)
    lgf = lg_ref[0, h]
    lgb = lg_ref[1, h]
    ii = lax.broadcasted_iota(jnp.int32, (c, c), 0)
    jj = lax.broadcasted_iota(jnp.int32, (c, c), 1)
    rel = (ii - jj).astype(F32)
    dm_f = jnp.where(rel >= 0, jnp.exp(lgf * jnp.maximum(rel, 0.0)), 0.0)
    dm_b = jnp.where(rel < 0, jnp.exp(lgb * jnp.maximum(-rel, 0.0)), 0.0)
    pos = lax.broadcasted_iota(jnp.int32, (c, HEAD_DIM), 0).astype(F32)
    qd_f = jnp.exp(lgf * (pos + 1.0))
    kd_f = jnp.exp(lgf * (c - 1.0 - pos))
    qd_b = jnp.exp(lgb * (c - pos))
    kd_b = jnp.exp(lgb * pos)
    cd_f = jnp.exp(jnp.full((1, RET_V_DIM), lgf * c, F32))
    cd_b = jnp.exp(jnp.full((1, RET_V_DIM), lgb * c, F32))
    gn = gn_ref[0]

    def rows(ci):
        return slice(ci * c, (ci + 1) * c)

    def wide(ref, ci):
        return jnp.concatenate([ref[0, rows(ci), :], ref[1, rows(ci), :]], axis=1)

    def step(q, k, v, state, dm, qd, kd, cd):
        kv = lax.dot_general((k.astype(F32) * kd).astype(BF16), v, (((0,), (0,)), ((), ())),
                             preferred_element_type=F32)
        out = None
        if q is not None:
            s = lax.dot_general(q, k, (((1,), (1,)), ((), ())), preferred_element_type=F32) * dm
            out = jnp.dot(s.astype(BF16), v, preferred_element_type=F32)
            if state is not None:
                out = out + jnp.dot((q.astype(F32) * qd).astype(BF16), state.astype(BF16),
                                    preferred_element_type=F32)
        return out, (kv if state is None else state * cd + kv)

    def finish(y, gate):
        mu = jnp.mean(y, axis=-1, keepdims=True)
        dlt = y - mu
        var = jnp.mean(dlt * dlt, axis=-1, keepdims=True)
        yn = dlt * lax.rsqrt(var + EPS) * gn
        gf = gate.astype(F32)
        return ((gf * jax.nn.sigmoid(gf)) * yn).astype(BF16)

    n_y_ctx = n_ctx if ctx_out else 0

    state = None
    for ci in range(n_ctx):
        q = cq_ref[0, rows(ci), :] if ctx_out else None
        out, state = step(q, ck_ref[0, rows(ci), :], wide(cv_ref, ci), state, dm_f, qd_f, kd_f, cd_f)
        if ctx_out:
            yf_ref[rows(ci), :] = out
    for ci in range(n_lat):
        out, state = step(q_ref[0, rows(ci), :], k_ref[0, rows(ci), :], wide(v_ref, ci), state,
                          dm_f, qd_f, kd_f, cd_f)
        yf_ref[rows(n_y_ctx + ci), :] = out

    state = None
    for ci in reversed(range(n_ctx)):
        q = cq_ref[0, rows(ci), :] if ctx_out else None
        out, state = step(q, ck_ref[0, rows(ci), :], wide(cv_ref, ci), state, dm_b, qd_b, kd_b, cd_b)
        if ctx_out:
            co_ref[rows(ci), :] = finish(yf_ref[rows(ci), :] + out, wide(cg_ref, ci))
    for ci in reversed(range(n_lat)):
        out, state = step(q_ref[0, rows(ci), :], k_ref[0, rows(ci), :], wide(v_ref, ci), state,
                          dm_b, qd_b, kd_b, cd_b)
        o_ref[rows(ci), :] = finish(yf_ref[rows(n_y_ctx + ci), :] + out, wide(g_ref, ci))


def _retention(p_lat, p_ctx, log_decay, gn_g, layout, batch, seq_len, ctx_len, ctx_out):
    nh = layout.ret_heads
    rq, rk, rv, rg = layout.rq0, layout.rk0, layout.rv0 // 2, layout.rg0 // 2
    narrow = lambda base, ln: pl.BlockSpec((1, ln, HEAD_DIM), lambda b, h, lg: (base + h, b, 0))
    wide = lambda base, ln: pl.BlockSpec((2, ln, HEAD_DIM), lambda b, h, lg: (base + h, b, 0))
    in_specs = [narrow(rq, seq_len), narrow(rk, seq_len), wide(rv, seq_len), wide(rg, seq_len)]
    args = [p_lat] * 4
    if ctx_out:
        in_specs += [narrow(rq, ctx_len), narrow(rk, ctx_len), wide(rv, ctx_len), wide(rg, ctx_len)]
        args += [p_ctx] * 4
    else:
        in_specs += [narrow(rk, ctx_len), wide(rv, ctx_len)]
        args += [p_ctx] * 2
    in_specs.append(pl.BlockSpec((1, 1, RET_V_DIM), lambda b, h, lg: (h, 0, 0)))
    args.append(gn_g.reshape(nh, 1, RET_V_DIM))
    out_specs = [pl.BlockSpec((seq_len, RET_V_DIM), lambda b, h, lg: (b, h))]
    out_shape = [jax.ShapeDtypeStruct((batch * seq_len, nh * RET_V_DIM), BF16)]
    y_rows = seq_len
    if ctx_out:
        out_specs.append(pl.BlockSpec((ctx_len, RET_V_DIM), lambda b, h, lg: (b, h)))
        out_shape.append(jax.ShapeDtypeStruct((batch * ctx_len, nh * RET_V_DIM), BF16))
        y_rows += ctx_len
    outs = pl.pallas_call(
        functools.partial(_ret_body, n_lat=seq_len // RET_CHUNK, n_ctx=ctx_len // RET_CHUNK, ctx_out=ctx_out),
        grid_spec=pltpu.PrefetchScalarGridSpec(
            num_scalar_prefetch=1,
            grid=(batch, nh),
            in_specs=in_specs,
            out_specs=out_specs,
            scratch_shapes=[pltpu.VMEM((y_rows, RET_V_DIM), F32)]),
        out_shape=out_shape,
        compiler_params=_params("parallel", "parallel"),
        name="retention" if ctx_out else "retention_last",
    )(log_decay, *args)
    return (outs[0], outs[1]) if ctx_out else (outs[0], None)


def _residual_epilogue(xo_ref, x_ref, gt_ref, gpost_ref, nxt, tm):
    d = xo_ref.shape[1]
    blk = (BF16_ROWS // SUBLANES, SUBLANES, d)
    gt = gt_ref[...]
    gpost = gpost_ref[...][None]
    if nxt is not None:
        gpre_ref, sh_ref, sc_ref, hn_ref = nxt
        gpre = gpre_ref[...][None]
        sc1 = 1.0 + sc_ref[...]
        sh = sh_ref[...]

    def body(rb, carry):
        rws = pl.ds(pl.multiple_of(rb * BF16_ROWS, BF16_ROWS), BF16_ROWS)
        y = xo_ref[rws, :].reshape(blk)
        r = lax.rsqrt(jnp.mean(y * y, axis=-1, keepdims=True) + EPS)
        xn = x_ref[rws, :].reshape(blk) + gt * ((xo_ref[rws, :].reshape(blk) * r) * gpost)
        xo_ref[rws, :] = xn.reshape(BF16_ROWS, d)
        if nxt is not None:
            r2 = lax.rsqrt(jnp.mean(xn * xn, axis=-1, keepdims=True) + EPS)
            xn2 = xo_ref[rws, :].reshape(blk)
            hn_ref[rws, :] = (((xn2 * r2) * gpre) * sc1 + sh).reshape(BF16_ROWS, d).astype(BF16)
        return carry

    lax.fori_loop(0, tm // BF16_ROWS, body, 0, unroll=2)


def _outproj_body(a_ref, r_ref, wa_ref, wb_ref, x_ref, gt_ref, gpost_ref, gpre_ref, sh_ref, sc_ref,
                  xo_ref, hn_ref, *, nj, tn, tm):
    j = pl.program_id(1)
    y = (jnp.dot(a_ref[...], wa_ref[0], preferred_element_type=F32)
         + jnp.dot(r_ref[...], wb_ref[0], preferred_element_type=F32))
    for jj in range(nj):
        @pl.when(j == jj)
        def _():
            xo_ref[:, jj * tn:(jj + 1) * tn] = y

    @pl.when(j == nj - 1)
    def _():
        _residual_epilogue(xo_ref, x_ref, gt_ref, gpost_ref, (gpre_ref, sh_ref, sc_ref, hn_ref), tm)


def _resident(shape, index_map):
    return pl.BlockSpec(shape, index_map, pipeline_mode=pl.Buffered(1))


def _outproj(att, ret, w_out, li, x, gt, g_post, g_pre, sh, sc, group_of_tile, tm=512, tn=512):
    m, d = x.shape
    ka = att.shape[1]
    assert ret.shape[1] == ka and w_out.shape[1] == 2 * ka
    nj = d // tn
    mod_spec = pl.BlockSpec((1, SUBLANES, d), lambda i, j: (group_of_tile(i, tm), 0, 0))
    vec_spec = pl.BlockSpec((SUBLANES, d), lambda i, j: (0, 0))
    row_tile = lambda i, j: (i, 0)
    return pl.pallas_call(
        functools.partial(_outproj_body, nj=nj, tn=tn, tm=tm),
        grid=(m // tm, nj),
        in_specs=[pl.BlockSpec((tm, ka), row_tile), pl.BlockSpec((tm, ka), row_tile),
                  pl.BlockSpec((1, ka, tn), lambda i, j: (li, 0, j)),
                  pl.BlockSpec((1, ka, tn), lambda i, j: (li, 1, j)),
                  _resident((tm, d), row_tile), mod_spec, vec_spec, vec_spec, mod_spec, mod_spec],
        out_specs=[pl.BlockSpec((tm, d), row_tile), pl.BlockSpec((tm, d), row_tile)],
        out_shape=[jax.ShapeDtypeStruct((m, d), F32), jax.ShapeDtypeStruct((m, d), BF16)],
        compiler_params=_params("parallel", "arbitrary"),
        name="outproj",
    )(att, ret, w_out, w_out, x, gt, _rep8(g_post), _rep8(g_pre), sh, sc)


def _mlp_body(h_ref, wu_ref, wd_ref, x_ref, gt_ref, gpost_ref, *rest, nf, nt, tm, has_next):
    if has_next:
        gpre_ref, sh_ref, sc_ref, xo_ref, hn_ref = rest
        nxt = (gpre_ref, sh_ref, sc_ref, hn_ref)
    else:
        (xo_ref,) = rest
        nxt = None
    f = pl.program_id(1)

    @pl.when(f == 0)
    def _():
        xo_ref[...] = jnp.zeros_like(xo_ref)

    u = jnp.dot(h_ref[...], wu_ref[0], preferred_element_type=F32)
    a = jnp.maximum(u, 0.0)
    a = (a * a).astype(BF16)
    for nn in range(xo_ref.shape[1] // nt):
        cols = slice(nn * nt, (nn + 1) * nt)
        xo_ref[:, cols] += jnp.dot(a, wd_ref[0, :, cols], preferred_element_type=F32)

    @pl.when(f == nf - 1)
    def _():
        _residual_epilogue(xo_ref, x_ref, gt_ref, gpost_ref, nxt, tm)


def _mlp(h, w_up, w_down, li, x, gt, g_post, nxt, group_of_tile, tm=512, tf=512, nt=512):
    m, d = x.shape
    ff = w_up.shape[2]
    nf = ff // tf
    mod_spec = pl.BlockSpec((1, SUBLANES, d), lambda i, f: (group_of_tile(i, tm), 0, 0))
    vec_spec = pl.BlockSpec((SUBLANES, d), lambda i, f: (0, 0))
    row_tile = lambda i, f: (i, 0)
    in_specs = [_resident((tm, d), row_tile),
                pl.BlockSpec((1, d, tf), lambda i, f: (li, 0, f)),
                pl.BlockSpec((1, tf, d), lambda i, f: (li, f, 0)),
                _resident((tm, d), row_tile), mod_spec, vec_spec]
    args = [h, w_up, w_down, x, gt, _rep8(g_post)]
    out_specs = [pl.BlockSpec((tm, d), row_tile)]
    out_shape = [jax.ShapeDtypeStruct((m, d), F32)]
    if nxt is not None:
        g_pre, sh, sc = nxt
        in_specs += [vec_spec, mod_spec, mod_spec]
        args += [_rep8(g_pre), sh, sc]
        out_specs.append(pl.BlockSpec((tm, d), row_tile))
        out_shape.append(jax.ShapeDtypeStruct((m, d), BF16))
    outs = pl.pallas_call(
        functools.partial(_mlp_body, nf=nf, nt=nt, tm=tm, has_next=nxt is not None),
        grid=(m // tm, nf),
        in_specs=in_specs,
        out_specs=out_specs,
        out_shape=out_shape,
        compiler_params=_params("parallel", "arbitrary"),
        name="mlp" if nxt is not None else "mlp_last",
    )(*args)
    return (outs[0], outs[1]) if nxt is not None else (outs[0], None)


class _Layout:
    def __init__(self, d_model):
        self.attn_heads = d_model // (2 * HEAD_DIM)
        self.kv_heads = self.attn_heads // ATTN_GROUP
        self.ret_heads = d_model // (2 * RET_V_DIM)
        vchunks = RET_V_DIM // LANES
        self.aq0 = 0
        self.ak0 = self.aq0 + self.attn_heads
        self.av0 = self.ak0 + self.kv_heads
        self.rq0 = self.av0 + self.kv_heads
        self.rk0 = self.rq0 + self.ret_heads
        self.rv0 = self.rk0 + self.ret_heads
        self.rg0 = self.rv0 + vchunks * self.ret_heads
        self.n_chunks = self.rg0 + vchunks * self.ret_heads

    def kind_of_chunk(self, c):
        if c < self.ak0:
            return "q"
        if c < self.av0:
            return "rope"
        if c < self.rq0:
            return "none"
        if c < self.rk0:
            return "rope"
        if c < self.rv0:
            return "rk"
        return "none"


def _rope_tables(seq_len):
    pairs = HEAD_DIM // 4
    t = jnp.arange(seq_len)
    r = (t // GRID_W).astype(F32)
    cl = (t % GRID_W).astype(F32)
    inv = ROPE_BASE ** (-jnp.arange(pairs, dtype=F32) / pairs)
    ar = r[:, None] * inv
    ac = cl[:, None] * inv
    ang = jnp.concatenate([ar, ar, ac, ac], axis=-1)
    cos, sin = jnp.cos(ang), jnp.sin(ang)
    first = (jnp.arange(HEAD_DIM) % (2 * pairs)) < pairs
    return cos, jnp.where(first, -sin, 0.0), jnp.where(first, 0.0, sin)


def kernel(x, c, ctx, c_ctx, w_ada, b_ada, norm_g, w_in, attn_sink, ret_log_decay, ret_gn_g, w_out, w_up, w_down):
    batch, seq_len, d = x.shape
    ctx_len = ctx.shape[1]
    depth = w_in.shape[0]
    layout = _Layout(d)
    assert layout.n_chunks * LANES == w_in.shape[2]
    assert batch + 1 <= MOD_ROWS

    xf = x.reshape(batch * seq_len, d)
    cf = ctx.reshape(batch * ctx_len, d)

    c_all = jnp.concatenate([c, c_ctx[None, :], jnp.zeros((MOD_ROWS - batch - 1, d), F32)], axis=0)
    mod = _adaln(c_all, w_ada, b_ada)
    mod = mod.reshape(depth, MOD_ROWS, 6, 1, d).transpose(0, 2, 1, 3, 4)
    mod = jnp.broadcast_to(mod, (depth, 6, MOD_ROWS, SUBLANES, d))
    lat_group = lambda i, tm: (i * tm) // seq_len
    ctx_group = lambda i, tm: batch

    rope_tabs = _rope_tables(seq_len)
    g = ATTN_GROUP
    wi, wo, wu, wd = (w.astype(BF16) for w in (w_in, w_out, w_up, w_down))

    h = _prenorm(xf, norm_g[0, 0], mod[0, 0], mod[0, 1], lat_group)
    hc = _prenorm(cf, norm_g[0, 0], mod[0, 0], mod[0, 1], ctx_group)
    for li in range(depth):
        last = li == depth - 1
        sh1, sc1, gt1, sh2, sc2, gt2 = [mod[li, k] for k in range(6)]
        g_pre_mix, g_post_mix, g_pre_mlp, g_post_mlp = [norm_g[li, k] for k in range(4)]
        sink = attn_sink[li].reshape(layout.kv_heads, g)

        p_lat = _inproj(h, wi, li, layout, rope_tabs, seq_len)
        p_ctx = _inproj(hc, wi, li, layout, None, ctx_len)
        sink_lat = jnp.repeat(sink, ATTN_BLOCK, axis=1)[:, :, None]
        att = _attention(p_lat, p_ctx, sink_lat, layout, batch, seq_len, ctx_len)
        ret, cret = _retention(p_lat, p_ctx, ret_log_decay[li], ret_gn_g[li], layout,
                               batch, seq_len, ctx_len, ctx_out=not last)
        xf, h2 = _outproj(att, ret, wo, li, xf, gt1, g_post_mix, g_pre_mlp, sh2, sc2, lat_group)
        nxt = None if last else (norm_g[li + 1, 0], mod[li + 1, 0], mod[li + 1, 1])
        xf, h = _mlp(h2, wu, wd, li, xf, gt2, g_post_mlp, nxt, lat_group)
        if not last:
            sink_ctx = jnp.repeat(sink, ctx_len, axis=1)[:, :, None]
            catt = _ctx_attention(p_ctx, sink_ctx, layout, batch, ctx_len)
            cf, hc2 = _outproj(catt, cret, wo, li, cf, gt1, g_post_mix, g_pre_mlp, sh2, sc2, ctx_group)
            cf, hc = _mlp(hc2, wu, wd, li, cf, gt2, g_post_mlp, nxt, ctx_group)
    return xf.reshape(batch, seq_len, d)
```

```python
import functools

import jax
import jax.numpy as jnp
from jax import lax
from jax.experimental import pallas as pl
from jax.experimental.pallas import tpu as pltpu

F32 = jnp.float32
BF16 = jnp.bfloat16

HEAD_DIM = 128
GRID_W = 64
WINDOW = 128
ATTN_BLOCK = 128
ATTN_GROUP = 4
RET_CHUNK = 128
RET_V_DIM = 2 * HEAD_DIM
ROPE_BASE = 10000.0
EPS = 1e-6
MASKED = -1e30

LANES = 128
SUBLANES = 8
BF16_ROWS = 16
MOD_ROWS = 16
VMEM_LIMIT = 56 * 1024 * 1024


def _rep8(v):
    return jnp.broadcast_to(v.reshape(1, -1), (SUBLANES, v.shape[-1]))


def _params(*sem):
    return pltpu.CompilerParams(dimension_semantics=sem, vmem_limit_bytes=VMEM_LIMIT)


def _adaln_body(c_ref, w_ref, b_ref, o_ref):
    c = c_ref[...]
    a = (c * jax.nn.sigmoid(c)).astype(BF16)
    o_ref[0] = jnp.dot(a, w_ref[0].astype(BF16), preferred_element_type=F32) + b_ref[0]


def _adaln(c_all, w_ada, b_ada, tn=512):
    depth, d, n = w_ada.shape
    rows = c_all.shape[0]
    return pl.pallas_call(
        _adaln_body,
        grid=(depth, n // tn),
        in_specs=[pl.BlockSpec((rows, d), lambda l, j: (0, 0)),
                  pl.BlockSpec((1, d, tn), lambda l, j: (l, 0, j)),
                  pl.BlockSpec((1, 1, tn), lambda l, j: (l, 0, j))],
        out_specs=pl.BlockSpec((1, rows, tn), lambda l, j: (l, 0, j)),
        out_shape=jax.ShapeDtypeStruct((depth, rows, n), F32),
        compiler_params=_params("parallel", "parallel"),
        name="adaln",
    )(c_all, w_ada, b_ada.reshape(depth, 1, n))


def _prenorm_body(x_ref, g_ref, sh_ref, sc_ref, o_ref):
    tm, d = x_ref.shape
    x = x_ref[...].reshape(tm // SUBLANES, SUBLANES, d)
    r = lax.rsqrt(jnp.mean(x * x, axis=-1, keepdims=True) + EPS)
    h = ((x * r) * g_ref[...][None]) * (1.0 + sc_ref[...]) + sh_ref[...]
    o_ref[...] = h.reshape(tm, d).astype(BF16)


def _prenorm(x, g, sh, sc, group_of_tile, tm=256):
    m, d = x.shape
    mod_spec = pl.BlockSpec((1, SUBLANES, d), lambda i: (group_of_tile(i, tm), 0, 0))
    return pl.pallas_call(
        _prenorm_body,
        grid=(m // tm,),
        in_specs=[pl.BlockSpec((tm, d), lambda i: (i, 0)),
                  pl.BlockSpec((SUBLANES, d), lambda i: (0, 0)),
                  mod_spec, mod_spec],
        out_specs=pl.BlockSpec((tm, d), lambda i: (i, 0)),
        out_shape=jax.ShapeDtypeStruct((m, d), BF16),
        compiler_params=_params("parallel"),
        name="prenorm",
    )(x, _rep8(g), sh, sc)


def _inproj_body(h_ref, w_ref, *rest, kinds, rope, n_chunk, q_scale, rk_scale):
    if rope:
        cos_ref, sa_ref, sb_ref, o_ref = rest
    else:
        (o_ref,) = rest
    j = pl.program_id(1)
    acc = jnp.dot(h_ref[...], w_ref[0], preferred_element_type=F32)

    def rot(t):
        if not rope:
            return t
        return t * cos_ref[...] + (pltpu.roll(t, 96, 1) * sa_ref[...] + pltpu.roll(t, 32, 1) * sb_ref[...])

    fns = {
        "q": lambda t: rot(t) * q_scale,
        "rope": rot,
        "rk": lambda t: rot(t * rk_scale),
        "none": lambda t: t,
    }
    for kind in sorted(set(kinds)):
        tiles = [t for t, k in enumerate(kinds) if k == kind]
        cond = functools.reduce(jnp.logical_or, [j == t for t in tiles])

        @pl.when(cond)
        def _():
            for cc in range(n_chunk):
                o_ref[cc] = fns[kind](acc[:, cc * LANES:(cc + 1) * LANES]).astype(BF16)


def _inproj(h, w, li, layout, rope_tabs, seq_len, tm=1024, tn=512):
    m, d = h.shape
    n = w.shape[2]
    n_chunk = tn // LANES
    kinds = []
    for t in range(n // tn):
        ks = {layout.kind_of_chunk(c) for c in range(t * n_chunk, (t + 1) * n_chunk)}
        assert len(ks) == 1, "column tile must not straddle head groups"
        kinds.append(ks.pop())
    rope = rope_tabs is not None
    in_specs = [pl.BlockSpec((tm, d), lambda i, j: (i, 0)),
                pl.BlockSpec((1, d, tn), lambda i, j: (li, 0, j))]
    args = [h, w]
    if rope:
        per_seq = seq_len // tm
        tab_spec = pl.BlockSpec((tm, LANES), lambda i, j: (i % per_seq, 0))
        in_specs += [tab_spec] * 3
        args += list(rope_tabs)
    return pl.pallas_call(
        functools.partial(_inproj_body, kinds=tuple(kinds), rope=rope, n_chunk=n_chunk,
                          q_scale=HEAD_DIM ** -0.5, rk_scale=HEAD_DIM ** -0.5),
        grid=(m // tm, n // tn),
        in_specs=in_specs,
        out_specs=pl.BlockSpec((n_chunk, tm, LANES), lambda i, j: (j, i, 0)),
        out_shape=jax.ShapeDtypeStruct((n // LANES, m, LANES), BF16),
        compiler_params=_params("parallel", "arbitrary"),
        name="inproj_rope" if rope else "inproj_ctx",
    )(*args)


def _attn_body(*refs, local, tq, kvh):
    if local:
        q_ref, kp_ref, kc_ref, kn_ref, vp_ref, vc_ref, vn_ref, ck_ref, cv_ref, sink_ref, bias_ref, o_ref = refs
    else:
        q_ref, ck_ref, cv_ref, sink_ref, o_ref = refs
    g = q_ref.shape[0] // kvh
    rows = g * tq
    nt = (((1,), (1,)), ((), ()))
    tn = (((0,), (0,)), ((), ()))
    for h in range(kvh):
        q = q_ref[h * g:(h + 1) * g].reshape(rows, HEAD_DIM)
        sink = sink_ref[h]
        s_ctx = lax.dot_general(ck_ref[h], q, nt, preferred_element_type=F32)
        m = jnp.maximum(jnp.max(s_ctx, axis=0, keepdims=True), sink)
        if local:
            k_loc = jnp.concatenate([kp_ref[h], kc_ref[h], kn_ref[h]], axis=0)
            s_loc = lax.dot_general(k_loc, q, nt, preferred_element_type=F32) + bias_ref[0]
            m = jnp.maximum(m, jnp.max(s_loc, axis=0, keepdims=True))
        p_ctx = jnp.exp(s_ctx - m)
        denom = jnp.sum(p_ctx, axis=0, keepdims=True) + jnp.exp(sink - m)
        ot = lax.dot_general(cv_ref[h], p_ctx.astype(BF16), tn, preferred_element_type=F32)
        if local:
            p_loc = jnp.exp(s_loc - m)
            denom = denom + jnp.sum(p_loc, axis=0, keepdims=True)
            v_loc = jnp.concatenate([vp_ref[h], vc_ref[h], vn_ref[h]], axis=0)
            ot = ot + lax.dot_general(v_loc, p_loc.astype(BF16), tn, preferred_element_type=F32)
        o = (ot / denom).T
        for gi in range(g):
            col = (h * g + gi) * HEAD_DIM
            o_ref[:, col:col + HEAD_DIM] = o[gi * tq:(gi + 1) * tq].astype(BF16)


def _band_bias(tq, n_qblocks, group):
    assert n_qblocks >= 2
    qq = jnp.tile(jnp.arange(tq), group)[None, :]
    cc = jnp.arange(3 * tq)[:, None]
    off = cc - qq
    band = (off >= 0) & (off <= 2 * WINDOW)
    first = band & (cc >= tq)
    last = band & (cc < 2 * tq)
    return jnp.where(jnp.stack([first, band, last]), 0.0, MASKED).astype(F32)


def _attention(p_lat, p_ctx, sink_row, layout, batch, seq_len, ctx_len):
    g, tq = ATTN_GROUP, ATTN_BLOCK
    nq = seq_len // tq
    kvh = layout.kv_heads
    assert layout.aq0 == 0 and layout.ak0 % kvh == 0 and layout.av0 % kvh == 0
    ak, av = layout.ak0 // kvh, layout.av0 // kvh
    prev = lambda b, n: b * nq + jnp.maximum(n - 1, 0)
    cur = lambda b, n: b * nq + n
    nxt = lambda b, n: b * nq + jnp.minimum(n + 1, nq - 1)
    in_specs = [pl.BlockSpec((kvh * g, tq, HEAD_DIM), lambda b, n: (0, cur(b, n), 0))]
    for base in (ak, av):
        for row in (prev, cur, nxt):
            in_specs.append(pl.BlockSpec((kvh, tq, HEAD_DIM), functools.partial(
                lambda b, n, base, row: (base, row(b, n), 0), base=base, row=row)))
    edge = lambda b, n: jnp.where(n == 0, 0, jnp.where(n == nq - 1, 2, 1))
    in_specs += [pl.BlockSpec((kvh, ctx_len, HEAD_DIM), lambda b, n: (ak, b, 0)),
                 pl.BlockSpec((kvh, ctx_len, HEAD_DIM), lambda b, n: (av, b, 0)),
                 pl.BlockSpec((kvh, 1, g * tq), lambda b, n: (0, 0, 0)),
                 pl.BlockSpec((1, 3 * tq, g * tq), lambda b, n: (edge(b, n), 0, 0))]
    return pl.pallas_call(
        functools.partial(_attn_body, local=True, tq=tq, kvh=kvh),
        grid=(batch, nq),
        in_specs=in_specs,
        out_specs=pl.BlockSpec((tq, kvh * g * HEAD_DIM), lambda b, n: (cur(b, n), 0)),
        out_shape=jax.ShapeDtypeStruct((batch * seq_len, kvh * g * HEAD_DIM), BF16),
        compiler_params=_params("parallel", "parallel"),
        name="window_attention",
    )(p_lat, *([p_lat] * 6), p_ctx, p_ctx, sink_row, _band_bias(tq, nq, g))


def _ctx_attention(p_ctx, sink_row, layout, batch, ctx_len):
    g = ATTN_GROUP
    kvh = layout.kv_heads
    ak, av = layout.ak0 // kvh, layout.av0 // kvh
    return pl.pallas_call(
        functools.partial(_attn_body, local=False, tq=ctx_len, kvh=kvh),
        grid=(batch,),
        in_specs=[pl.BlockSpec((kvh * g, ctx_len, HEAD_DIM), lambda b: (0, b, 0)),
                  pl.BlockSpec((kvh, ctx_len, HEAD_DIM), lambda b: (ak, b, 0)),
                  pl.BlockSpec((kvh, ctx_len, HEAD_DIM), lambda b: (av, b, 0)),
                  pl.BlockSpec((kvh, 1, g * ctx_len), lambda b: (0, 0, 0))],
        out_specs=pl.BlockSpec((ctx_len, kvh * g * HEAD_DIM), lambda b: (b, 0)),
        out_shape=jax.ShapeDtypeStruct((batch * ctx_len, kvh * g * HEAD_DIM), BF16),
        compiler_params=_params("parallel"),
        name="context_attention",
    )(p_ctx, p_ctx, p_ctx, sink_row)


def _ret_body(lg_ref, q_ref, k_ref, v_ref, g_ref, *rest, n_lat, n_ctx, ctx_out):
    if ctx_out:
        cq_ref, ck_ref, cv_ref, cg_ref, gn_ref, o_ref, co_ref, yf_ref = rest
    else:
        ck_ref, cv_ref, gn_ref, o_ref, yf_ref = rest
        cq_ref = cg_ref = co_ref = None
    c = RET_CHUNK
    h = pl.program_id(1)
    lgf = lg_ref[0, h]
    lgb = lg_ref[1, h]
    ii = lax.broadcasted_iota(jnp.int32, (c, c), 0)
    jj = lax.broadcasted_iota(jnp.int32, (c, c), 1)
    rel = (ii - jj).astype(F32)
    dm_f = jnp.where(rel >= 0, jnp.exp(lgf * jnp.maximum(rel, 0.0)), 0.0)
    dm_b = jnp.where(rel < 0, jnp.exp(lgb * jnp.maximum(-rel, 0.0)), 0.0)
    pos = lax.broadcasted_iota(jnp.int32, (c, HEAD_DIM), 0).astype(F32)
    qd_f = jnp.exp(lgf * (pos + 1.0))
    kd_f = jnp.exp(lgf * (c - 1.0 - pos))
    qd_b = jnp.exp(lgb * (c - pos))
    kd_b = jnp.exp(lgb * pos)
    cd_f = jnp.exp(jnp.full((1, RET_V_DIM), lgf * c, F32))
    cd_b = jnp.exp(jnp.full((1, RET_V_DIM), lgb * c, F32))
    gn = gn_ref[0]

    def rows(ci):
        return slice(ci * c, (ci + 1) * c)

    def wide(ref, ci):
        return jnp.concatenate([ref[0, rows(ci), :], ref[1, rows(ci), :]], axis=1)

    def step(q, k, v, state, dm, qd, kd, cd):
        kv = lax.dot_general((k.astype(F32) * kd).astype(BF16), v, (((0,), (0,)), ((), ())),
                             preferred_element_type=F32)
        out = None
        if q is not None:
            s = lax.dot_general(q, k, (((1,), (1,)), ((), ())), preferred_element_type=F32) * dm
            out = jnp.dot(s.astype(BF16), v, preferred_element_type=F32)
            if state is not None:
                out = out + jnp.dot((q.astype(F32) * qd).astype(BF16), state.astype(BF16),
                                    preferred_element_type=F32)
        return out, (kv if state is None else state * cd + kv)

    def finish(y, gate):
        mu = jnp.mean(y, axis=-1, keepdims=True)
        dlt = y - mu
        var = jnp.mean(dlt * dlt, axis=-1, keepdims=True)
        yn = dlt * lax.rsqrt(var + EPS) * gn
        gf = gate.astype(F32)
        return ((gf * jax.nn.sigmoid(gf)) * yn).astype(BF16)

    n_y_ctx = n_ctx if ctx_out else 0

    state = None
    for ci in range(n_ctx):
        q = cq_ref[0, rows(ci), :] if ctx_out else None
        out, state = step(q, ck_ref[0, rows(ci), :], wide(cv_ref, ci), state, dm_f, qd_f, kd_f, cd_f)
        if ctx_out:
            yf_ref[rows(ci), :] = out
    for ci in range(n_lat):
        out, state = step(q_ref[0, rows(ci), :], k_ref[0, rows(ci), :], wide(v_ref, ci), state,
                          dm_f, qd_f, kd_f, cd_f)
        yf_ref[rows(n_y_ctx + ci), :] = out

    state = None
    for ci in reversed(range(n_ctx)):
        q = cq_ref[0, rows(ci), :] if ctx_out else None
        out, state = step(q, ck_ref[0, rows(ci), :], wide(cv_ref, ci), state, dm_b, qd_b, kd_b, cd_b)
        if ctx_out:
            co_ref[rows(ci), :] = finish(yf_ref[rows(ci), :] + out, wide(cg_ref, ci))
    for ci in reversed(range(n_lat)):
        out, state = step(q_ref[0, rows(ci), :], k_ref[0, rows(ci), :], wide(v_ref, ci), state,
                          dm_b, qd_b, kd_b, cd_b)
        o_ref[rows(ci), :] = finish(yf_ref[rows(n_y_ctx + ci), :] + out, wide(g_ref, ci))


def _retention(p_lat, p_ctx, log_decay, gn_g, layout, batch, seq_len, ctx_len, ctx_out):
    nh = layout.ret_heads
    rq, rk, rv, rg = layout.rq0, layout.rk0, layout.rv0 // 2, layout.rg0 // 2
    narrow = lambda base, ln: pl.BlockSpec((1, ln, HEAD_DIM), lambda b, h, lg: (base + h, b, 0))
    wide = lambda base, ln: pl.BlockSpec((2, ln, HEAD_DIM), lambda b, h, lg: (base + h, b, 0))
    in_specs = [narrow(rq, seq_len), narrow(rk, seq_len), wide(rv, seq_len), wide(rg, seq_len)]
    args = [p_lat] * 4
    if ctx_out:
        in_specs += [narrow(rq, ctx_len), narrow(rk, ctx_len), wide(rv, ctx_len), wide(rg, ctx_len)]
        args += [p_ctx] * 4
    else:
        in_specs += [narrow(rk, ctx_len), wide(rv, ctx_len)]
        args += [p_ctx] * 2
    in_specs.append(pl.BlockSpec((1, 1, RET_V_DIM), lambda b, h, lg: (h, 0, 0)))
    args.append(gn_g.reshape(nh, 1, RET_V_DIM))
    out_specs = [pl.BlockSpec((seq_len, RET_V_DIM), lambda b, h, lg: (b, h))]
    out_shape = [jax.ShapeDtypeStruct((batch * seq_len, nh * RET_V_DIM), BF16)]
    y_rows = seq_len
    if ctx_out:
        out_specs.append(pl.BlockSpec((ctx_len, RET_V_DIM), lambda b, h, lg: (b, h)))
        out_shape.append(jax.ShapeDtypeStruct((batch * ctx_len, nh * RET_V_DIM), BF16))
        y_rows += ctx_len
    outs = pl.pallas_call(
        functools.partial(_ret_body, n_lat=seq_len // RET_CHUNK, n_ctx=ctx_len // RET_CHUNK, ctx_out=ctx_out),
        grid_spec=pltpu.PrefetchScalarGridSpec(
            num_scalar_prefetch=1,
            grid=(batch, nh),
            in_specs=in_specs,
            out_specs=out_specs,
            scratch_shapes=[pltpu.VMEM((y_rows, RET_V_DIM), F32)]),
        out_shape=out_shape,
        compiler_params=_params("parallel", "parallel"),
        name="retention" if ctx_out else "retention_last",
    )(log_decay, *args)
    return (outs[0], outs[1]) if ctx_out else (outs[0], None)


def _residual_epilogue(xo_ref, x_ref, gt_ref, gpost_ref, nxt, part_ref, tm):
    d = xo_ref.shape[1]
    nc = d // LANES
    blk = (BF16_ROWS // SUBLANES, SUBLANES, LANES)
    n_blocks = tm // BF16_ROWS

    def rows_of(rb):
        return pl.ds(pl.multiple_of(rb * BF16_ROWS, BF16_ROWS), BF16_ROWS)

    def cols(c):
        return slice(c * LANES, (c + 1) * LANES)

    def partials_to_rsqrt():
        ss = jnp.sum(part_ref[...], axis=-1, keepdims=True)
        part_ref[...] = jnp.broadcast_to(lax.rsqrt(ss / d + EPS), part_ref.shape)

    def sweep_sumsq(rb, carry):
        rws = rows_of(rb)
        acc = [jnp.zeros((BF16_ROWS, LANES), F32), jnp.zeros((BF16_ROWS, LANES), F32)]
        for c in range(nc):
            y = xo_ref[rws, cols(c)]
            acc[c % 2] = acc[c % 2] + y * y
        part_ref[rws, :] = acc[0] + acc[1]
        return carry

    def sweep_residual(rb, carry):
        rws = rows_of(rb)
        r = part_ref[rws, :].reshape(blk)
        acc = [jnp.zeros(blk, F32), jnp.zeros(blk, F32)]
        for c in range(nc):
            y = xo_ref[rws, cols(c)].reshape(blk)
            xn = x_ref[rws, cols(c)].reshape(blk) + gt_ref[:, :, cols(c)] * ((y * r) * gpost_ref[:, cols(c)][None])
            xo_ref[rws, cols(c)] = xn.reshape(BF16_ROWS, LANES)
            acc[c % 2] = acc[c % 2] + xn * xn
        part_ref[rws, :] = (acc[0] + acc[1]).reshape(BF16_ROWS, LANES)
        return carry

    lax.fori_loop(0, n_blocks, sweep_sumsq, 0, unroll=2)
    partials_to_rsqrt()
    lax.fori_loop(0, n_blocks, sweep_residual, 0, unroll=2)
    if nxt is None:
        return
    gpre_ref, sh_ref, sc_ref, hn_ref = nxt
    sc1 = 1.0 + sc_ref[...]

    def sweep_prenorm(rb, carry):
        rws = rows_of(rb)
        r2 = part_ref[rws, :].reshape(blk)
        for c in range(nc):
            xn = xo_ref[rws, cols(c)].reshape(blk)
            hn = ((xn * r2) * gpre_ref[:, cols(c)][None]) * sc1[:, :, cols(c)] + sh_ref[:, :, cols(c)]
            hn_ref[rws, cols(c)] = hn.reshape(BF16_ROWS, LANES).astype(BF16)
        return carry

    partials_to_rsqrt()
    lax.fori_loop(0, n_blocks, sweep_prenorm, 0, unroll=2)


def _outproj_body(a_ref, r_ref, wa_ref, wb_ref, x_ref, gt_ref, gpost_ref, gpre_ref, sh_ref, sc_ref,
                  xo_ref, hn_ref, part_ref, *, nj, tn, tm):
    j = pl.program_id(1)
    y = (jnp.dot(a_ref[...], wa_ref[0], preferred_element_type=F32)
         + jnp.dot(r_ref[...], wb_ref[0], preferred_element_type=F32))
    for jj in range(nj):
        @pl.when(j == jj)
        def _():
            xo_ref[:, jj * tn:(jj + 1) * tn] = y

    @pl.when(j == nj - 1)
    def _():
        _residual_epilogue(xo_ref, x_ref, gt_ref, gpost_ref, (gpre_ref, sh_ref, sc_ref, hn_ref), part_ref, tm)


def _resident(shape, index_map):
    return pl.BlockSpec(shape, index_map, pipeline_mode=pl.Buffered(1))


def _outproj(att, ret, w_out, li, x, gt, g_post, g_pre, sh, sc, group_of_tile, tm=512, tn=512):
    m, d = x.shape
    ka = att.shape[1]
    assert ret.shape[1] == ka and w_out.shape[1] == 2 * ka
    nj = d // tn
    mod_spec = pl.BlockSpec((1, SUBLANES, d), lambda i, j: (group_of_tile(i, tm), 0, 0))
    vec_spec = pl.BlockSpec((SUBLANES, d), lambda i, j: (0, 0))
    row_tile = lambda i, j: (i, 0)
    return pl.pallas_call(
        functools.partial(_outproj_body, nj=nj, tn=tn, tm=tm),
        grid=(m // tm, nj),
        in_specs=[pl.BlockSpec((tm, ka), row_tile), pl.BlockSpec((tm, ka), row_tile),
                  pl.BlockSpec((1, ka, tn), lambda i, j: (li, 0, j)),
                  pl.BlockSpec((1, ka, tn), lambda i, j: (li, 1, j)),
                  _resident((tm, d), row_tile), mod_spec, vec_spec, vec_spec, mod_spec, mod_spec],
        out_specs=[pl.BlockSpec((tm, d), row_tile), pl.BlockSpec((tm, d), row_tile)],
        out_shape=[jax.ShapeDtypeStruct((m, d), F32), jax.ShapeDtypeStruct((m, d), BF16)],
        scratch_shapes=[pltpu.VMEM((tm, LANES), F32)],
        compiler_params=_params("parallel", "arbitrary"),
        name="outproj",
    )(att, ret, w_out, w_out, x, gt, _rep8(g_post), _rep8(g_pre), sh, sc)


def _mlp_body(h_ref, wu_ref, wd_ref, x_ref, gt_ref, gpost_ref, *rest, nf, nt, tm, has_next):
    if has_next:
        gpre_ref, sh_ref, sc_ref, xo_ref, hn_ref, part_ref = rest
        nxt = (gpre_ref, sh_ref, sc_ref, hn_ref)
    else:
        xo_ref, part_ref = rest
        nxt = None
    f = pl.program_id(1)

    @pl.when(f == 0)
    def _():
        xo_ref[...] = jnp.zeros_like(xo_ref)

    u = jnp.dot(h_ref[...], wu_ref[0], preferred_element_type=F32)
    a = jnp.maximum(u, 0.0)
    a = (a * a).astype(BF16)
    for nn in range(xo_ref.shape[1] // nt):
        cols = slice(nn * nt, (nn + 1) * nt)
        xo_ref[:, cols] += jnp.dot(a, wd_ref[0, :, cols], preferred_element_type=F32)

    @pl.when(f == nf - 1)
    def _():
        _residual_epilogue(xo_ref, x_ref, gt_ref, gpost_ref, nxt, part_ref, tm)


def _mlp(h, w_up, w_down, li, x, gt, g_post, nxt, group_of_tile, tm=512, tf=512, nt=512):
    m, d = x.shape
    ff = w_up.shape[2]
    nf = ff // tf
    mod_spec = pl.BlockSpec((1, SUBLANES, d), lambda i, f: (group_of_tile(i, tm), 0, 0))
    vec_spec = pl.BlockSpec((SUBLANES, d), lambda i, f: (0, 0))
    row_tile = lambda i, f: (i, 0)
    in_specs = [_resident((tm, d), row_tile),
                pl.BlockSpec((1, d, tf), lambda i, f: (li, 0, f)),
                pl.BlockSpec((1, tf, d), lambda i, f: (li, f, 0)),
                _resident((tm, d), row_tile), mod_spec, vec_spec]
    args = [h, w_up, w_down, x, gt, _rep8(g_post)]
    out_specs = [pl.BlockSpec((tm, d), row_tile)]
    out_shape = [jax.ShapeDtypeStruct((m, d), F32)]
    if nxt is not None:
        g_pre, sh, sc = nxt
        in_specs += [vec_spec, mod_spec, mod_spec]
        args += [_rep8(g_pre), sh, sc]
        out_specs.append(pl.BlockSpec((tm, d), row_tile))
        out_shape.append(jax.ShapeDtypeStruct((m, d), BF16))
    outs = pl.pallas_call(
        functools.partial(_mlp_body, nf=nf, nt=nt, tm=tm, has_next=nxt is not None),
        grid=(m // tm, nf),
        in_specs=in_specs,
        out_specs=out_specs,
        out_shape=out_shape,
        scratch_shapes=[pltpu.VMEM((tm, LANES), F32)],
        compiler_params=_params("parallel", "arbitrary"),
        name="mlp" if nxt is not None else "mlp_last",
    )(*args)
    return (outs[0], outs[1]) if nxt is not None else (outs[0], None)


class _Layout:
    def __init__(self, d_model):
        self.attn_heads = d_model // (2 * HEAD_DIM)
        self.kv_heads = self.attn_heads // ATTN_GROUP
        self.ret_heads = d_model // (2 * RET_V_DIM)
        vchunks = RET_V_DIM // LANES
        self.aq0 = 0
        self.ak0 = self.aq0 + self.attn_heads
        self.av0 = self.ak0 + self.kv_heads
        self.rq0 = self.av0 + self.kv_heads
        self.rk0 = self.rq0 + self.ret_heads
        self.rv0 = self.rk0 + self.ret_heads
        self.rg0 = self.rv0 + vchunks * self.ret_heads
        self.n_chunks = self.rg0 + vchunks * self.ret_heads

    def kind_of_chunk(self, c):
        if c < self.ak0:
            return "q"
        if c < self.av0:
            return "rope"
        if c < self.rq0:
            return "none"
        if c < self.rk0:
            return "rope"
        if c < self.rv0:
            return "rk"
        return "none"


def _rope_tables(seq_len):
    pairs = HEAD_DIM // 4
    t = jnp.arange(seq_len)
    r = (t // GRID_W).astype(F32)
    cl = (t % GRID_W).astype(F32)
    inv = ROPE_BASE ** (-jnp.arange(pairs, dtype=F32) / pairs)
    ar = r[:, None] * inv
    ac = cl[:, None] * inv
    ang = jnp.concatenate([ar, ar, ac, ac], axis=-1)
    cos, sin = jnp.cos(ang), jnp.sin(ang)
    first = (jnp.arange(HEAD_DIM) % (2 * pairs)) < pairs
    return cos, jnp.where(first, -sin, 0.0), jnp.where(first, 0.0, sin)


def kernel(x, c, ctx, c_ctx, w_ada, b_ada, norm_g, w_in, attn_sink, ret_log_decay, ret_gn_g, w_out, w_up, w_down):
    batch, seq_len, d = x.shape
    ctx_len = ctx.shape[1]
    depth = w_in.shape[0]
    layout = _Layout(d)
    assert layout.n_chunks * LANES == w_in.shape[2]
    assert batch + 1 <= MOD_ROWS

    xf = x.reshape(batch * seq_len, d)
    cf = ctx.reshape(batch * ctx_len, d)

    c_all = jnp.concatenate([c, c_ctx[None, :], jnp.zeros((MOD_ROWS - batch - 1, d), F32)], axis=0)
    mod = _adaln(c_all, w_ada, b_ada)
    mod = mod.reshape(depth, MOD_ROWS, 6, 1, d).transpose(0, 2, 1, 3, 4)
    mod = jnp.broadcast_to(mod, (depth, 6, MOD_ROWS, SUBLANES, d))
    lat_group = lambda i, tm: (i * tm) // seq_len
    ctx_group = lambda i, tm: batch

    rope_tabs = _rope_tables(seq_len)
    g = ATTN_GROUP
    wi, wo, wu, wd = (w.astype(BF16) for w in (w_in, w_out, w_up, w_down))

    h = _prenorm(xf, norm_g[0, 0], mod[0, 0], mod[0, 1], lat_group)
    hc = _prenorm(cf, norm_g[0, 0], mod[0, 0], mod[0, 1], ctx_group)
    for li in range(depth):
        last = li == depth - 1
        sh1, sc1, gt1, sh2, sc2, gt2 = [mod[li, k] for k in range(6)]
        g_pre_mix, g_post_mix, g_pre_mlp, g_post_mlp = [norm_g[li, k] for k in range(4)]
        sink = attn_sink[li].reshape(layout.kv_heads, g)

        p_lat = _inproj(h, wi, li, layout, rope_tabs, seq_len)
        p_ctx = _inproj(hc, wi, li, layout, None, ctx_len)
        sink_lat = jnp.repeat(sink, ATTN_BLOCK, axis=1)[:, None, :]
        att = _attention(p_lat, p_ctx, sink_lat, layout, batch, seq_len, ctx_len)
        ret, cret = _retention(p_lat, p_ctx, ret_log_decay[li], ret_gn_g[li], layout,
                               batch, seq_len, ctx_len, ctx_out=not last)
        xf, h2 = _outproj(att, ret, wo, li, xf, gt1, g_post_mix, g_pre_mlp, sh2, sc2, lat_group)
        nxt = None if last else (norm_g[li + 1, 0], mod[li + 1, 0], mod[li + 1, 1])
        xf, h = _mlp(h2, wu, wd, li, xf, gt2, g_post_mlp, nxt, lat_group)
        if not last:
            sink_ctx = jnp.repeat(sink, ctx_len, axis=1)[:, None, :]
            catt = _ctx_attention(p_ctx, sink_ctx, layout, batch, ctx_len)
            cf, hc2 = _outproj(catt, cret, wo, li, cf, gt1, g_post_mix, g_pre_mlp, sh2, sc2, ctx_group)
            cf, hc = _mlp(hc2, wu, wd, li, cf, gt2, g_post_mlp, nxt, ctx_group)
    return xf.reshape(batch, seq_len, d)
```

```python
import functools

import jax
import jax.numpy as jnp
from jax import lax
from jax.experimental import pallas as pl
from jax.experimental.pallas import tpu as pltpu

F32 = jnp.float32
BF16 = jnp.bfloat16

HEAD_DIM = 128
GRID_W = 64
WINDOW = 128
ATTN_BLOCK = 128
ATTN_GROUP = 4
RET_CHUNK = 128
RET_V_DIM = 2 * HEAD_DIM
ROPE_BASE = 10000.0
EPS = 1e-6
MASKED = -1e30

LANES = 128
SUBLANES = 8
BF16_ROWS = 16
MOD_ROWS = 16
VMEM_LIMIT = 56 * 1024 * 1024
IN_TN = 512
OUT_TN = 512
FF_TN = 512


def _rep8(v):
    return jnp.broadcast_to(v.reshape(1, -1), (SUBLANES, v.shape[-1]))


def _column_tiles(w, tn):
    depth, k, n = w.shape
    return w.astype(BF16).reshape(depth, k, n // tn, tn).transpose(0, 2, 1, 3)


def _params(*sem):
    return pltpu.CompilerParams(dimension_semantics=sem, vmem_limit_bytes=VMEM_LIMIT)


def _adaln_body(c_ref, w_ref, b_ref, o_ref):
    c = c_ref[...]
    a = (c * jax.nn.sigmoid(c)).astype(BF16)
    o_ref[0] = jnp.dot(a, w_ref[0].astype(BF16), preferred_element_type=F32) + b_ref[0]


def _adaln(c_all, w_ada, b_ada, tn=512):
    depth, d, n = w_ada.shape
    rows = c_all.shape[0]
    return pl.pallas_call(
        _adaln_body,
        grid=(depth, n // tn),
        in_specs=[pl.BlockSpec((rows, d), lambda l, j: (0, 0)),
                  pl.BlockSpec((1, d, tn), lambda l, j: (l, 0, j)),
                  pl.BlockSpec((1, 1, tn), lambda l, j: (l, 0, j))],
        out_specs=pl.BlockSpec((1, rows, tn), lambda l, j: (l, 0, j)),
        out_shape=jax.ShapeDtypeStruct((depth, rows, n), F32),
        compiler_params=_params("parallel", "parallel"),
        name="adaln",
    )(c_all, w_ada, b_ada.reshape(depth, 1, n))


def _prenorm_body(x_ref, g_ref, sh_ref, sc_ref, o_ref):
    tm, d = x_ref.shape
    x = x_ref[...].reshape(tm // SUBLANES, SUBLANES, d)
    r = lax.rsqrt(jnp.mean(x * x, axis=-1, keepdims=True) + EPS)
    h = ((x * r) * g_ref[...][None]) * (1.0 + sc_ref[...]) + sh_ref[...]
    o_ref[...] = h.reshape(tm, d).astype(BF16)


def _prenorm(x, g, sh, sc, group_of_tile, tm=256):
    m, d = x.shape
    mod_spec = pl.BlockSpec((1, SUBLANES, d), lambda i: (group_of_tile(i, tm), 0, 0))
    return pl.pallas_call(
        _prenorm_body,
        grid=(m // tm,),
        in_specs=[pl.BlockSpec((tm, d), lambda i: (i, 0)),
                  pl.BlockSpec((SUBLANES, d), lambda i: (0, 0)),
                  mod_spec, mod_spec],
        out_specs=pl.BlockSpec((tm, d), lambda i: (i, 0)),
        out_shape=jax.ShapeDtypeStruct((m, d), BF16),
        compiler_params=_params("parallel"),
        name="prenorm",
    )(x, _rep8(g), sh, sc)


def _inproj_body(h_ref, w_ref, *rest, kinds, rope, n_chunk, q_scale, rk_scale):
    if rope:
        cos_ref, sa_ref, sb_ref, o_ref = rest
    else:
        (o_ref,) = rest
    j = pl.program_id(1)
    acc = jnp.dot(h_ref[...], w_ref[0, 0], preferred_element_type=F32)

    def rot(t):
        if not rope:
            return t
        return t * cos_ref[...] + (pltpu.roll(t, 96, 1) * sa_ref[...] + pltpu.roll(t, 32, 1) * sb_ref[...])

    fns = {
        "q": lambda t: rot(t) * q_scale,
        "rope": rot,
        "rk": lambda t: rot(t * rk_scale),
        "none": lambda t: t,
    }
    for kind in sorted(set(kinds)):
        tiles = [t for t, k in enumerate(kinds) if k == kind]
        cond = functools.reduce(jnp.logical_or, [j == t for t in tiles])

        @pl.when(cond)
        def _():
            for cc in range(n_chunk):
                o_ref[cc] = fns[kind](acc[:, cc * LANES:(cc + 1) * LANES]).astype(BF16)


def _inproj(h, w, li, layout, rope_tabs, seq_len, tm=1024):
    m, d = h.shape
    tn = w.shape[3]
    n = w.shape[1] * tn
    n_chunk = tn // LANES
    kinds = []
    for t in range(n // tn):
        ks = {layout.kind_of_chunk(c) for c in range(t * n_chunk, (t + 1) * n_chunk)}
        assert len(ks) == 1, "column tile must not straddle head groups"
        kinds.append(ks.pop())
    rope = rope_tabs is not None
    in_specs = [pl.BlockSpec((tm, d), lambda i, j: (i, 0)),
                pl.BlockSpec((1, 1, d, tn), lambda i, j: (li, j, 0, 0))]
    args = [h, w]
    if rope:
        per_seq = seq_len // tm
        tab_spec = pl.BlockSpec((tm, LANES), lambda i, j: (i % per_seq, 0))
        in_specs += [tab_spec] * 3
        args += list(rope_tabs)
    return pl.pallas_call(
        functools.partial(_inproj_body, kinds=tuple(kinds), rope=rope, n_chunk=n_chunk,
                          q_scale=HEAD_DIM ** -0.5, rk_scale=HEAD_DIM ** -0.5),
        grid=(m // tm, n // tn),
        in_specs=in_specs,
        out_specs=pl.BlockSpec((n_chunk, tm, LANES), lambda i, j: (j, i, 0)),
        out_shape=jax.ShapeDtypeStruct((n // LANES, m, LANES), BF16),
        compiler_params=_params("parallel", "arbitrary"),
        name="inproj_rope" if rope else "inproj_ctx",
    )(*args)


def _attn_body(*refs, local, tq, kvh):
    if local:
        q_ref, kp_ref, kc_ref, kn_ref, vp_ref, vc_ref, vn_ref, ck_ref, cv_ref, sink_ref, bias_ref, o_ref = refs
    else:
        q_ref, ck_ref, cv_ref, sink_ref, o_ref = refs
    g = q_ref.shape[0] // kvh
    rows = g * tq
    nt = (((1,), (1,)), ((), ()))
    tn = (((0,), (0,)), ((), ()))
    for h in range(kvh):
        q = q_ref[h * g:(h + 1) * g].reshape(rows, HEAD_DIM)
        sink = sink_ref[h]
        s_ctx = lax.dot_general(ck_ref[h], q, nt, preferred_element_type=F32)
        m = jnp.maximum(jnp.max(s_ctx, axis=0, keepdims=True), sink)
        if local:
            k_loc = jnp.concatenate([kp_ref[h], kc_ref[h], kn_ref[h]], axis=0)
            s_loc = lax.dot_general(k_loc, q, nt, preferred_element_type=F32) + bias_ref[0]
            m = jnp.maximum(m, jnp.max(s_loc, axis=0, keepdims=True))
        p_ctx = jnp.exp(s_ctx - m)
        denom = jnp.sum(p_ctx, axis=0, keepdims=True) + jnp.exp(sink - m)
        ot = lax.dot_general(cv_ref[h], p_ctx.astype(BF16), tn, preferred_element_type=F32)
        if local:
            p_loc = jnp.exp(s_loc - m)
            denom = denom + jnp.sum(p_loc, axis=0, keepdims=True)
            v_loc = jnp.concatenate([vp_ref[h], vc_ref[h], vn_ref[h]], axis=0)
            ot = ot + lax.dot_general(v_loc, p_loc.astype(BF16), tn, preferred_element_type=F32)
        o = (ot / denom).T
        for gi in range(g):
            col = (h * g + gi) * HEAD_DIM
            o_ref[:, col:col + HEAD_DIM] = o[gi * tq:(gi + 1) * tq].astype(BF16)


def _band_bias(tq, n_qblocks, group):
    assert n_qblocks >= 2
    qq = jnp.tile(jnp.arange(tq), group)[None, :]
    cc = jnp.arange(3 * tq)[:, None]
    off = cc - qq
    band = (off >= 0) & (off <= 2 * WINDOW)
    first = band & (cc >= tq)
    last = band & (cc < 2 * tq)
    return jnp.where(jnp.stack([first, band, last]), 0.0, MASKED).astype(F32)


def _attention(p_lat, p_ctx, sink_row, layout, batch, seq_len, ctx_len):
    g, tq = ATTN_GROUP, ATTN_BLOCK
    nq = seq_len // tq
    kvh = layout.kv_heads
    assert layout.aq0 == 0 and layout.ak0 % kvh == 0 and layout.av0 % kvh == 0
    ak, av = layout.ak0 // kvh, layout.av0 // kvh
    prev = lambda b, n: b * nq + jnp.maximum(n - 1, 0)
    cur = lambda b, n: b * nq + n
    nxt = lambda b, n: b * nq + jnp.minimum(n + 1, nq - 1)
    in_specs = [pl.BlockSpec((kvh * g, tq, HEAD_DIM), lambda b, n: (0, cur(b, n), 0))]
    for base in (ak, av):
        for row in (prev, cur, nxt):
            in_specs.append(pl.BlockSpec((kvh, tq, HEAD_DIM), functools.partial(
                lambda b, n, base, row: (base, row(b, n), 0), base=base, row=row)))
    edge = lambda b, n: jnp.where(n == 0, 0, jnp.where(n == nq - 1, 2, 1))
    in_specs += [pl.BlockSpec((kvh, ctx_len, HEAD_DIM), lambda b, n: (ak, b, 0)),
                 pl.BlockSpec((kvh, ctx_len, HEAD_DIM), lambda b, n: (av, b, 0)),
                 pl.BlockSpec((kvh, 1, g * tq), lambda b, n: (0, 0, 0)),
                 pl.BlockSpec((1, 3 * tq, g * tq), lambda b, n: (edge(b, n), 0, 0))]
    return pl.pallas_call(
        functools.partial(_attn_body, local=True, tq=tq, kvh=kvh),
        grid=(batch, nq),
        in_specs=in_specs,
        out_specs=pl.BlockSpec((tq, kvh * g * HEAD_DIM), lambda b, n: (cur(b, n), 0)),
        out_shape=jax.ShapeDtypeStruct((batch * seq_len, kvh * g * HEAD_DIM), BF16),
        compiler_params=_params("parallel", "parallel"),
        name="window_attention",
    )(p_lat, *([p_lat] * 6), p_ctx, p_ctx, sink_row, _band_bias(tq, nq, g))


def _ctx_attention(p_ctx, sink_row, layout, batch, ctx_len):
    g = ATTN_GROUP
    kvh = layout.kv_heads
    ak, av = layout.ak0 // kvh, layout.av0 // kvh
    return pl.pallas_call(
        functools.partial(_attn_body, local=False, tq=ctx_len, kvh=kvh),
        grid=(batch,),
        in_specs=[pl.BlockSpec((kvh * g, ctx_len, HEAD_DIM), lambda b: (0, b, 0)),
                  pl.BlockSpec((kvh, ctx_len, HEAD_DIM), lambda b: (ak, b, 0)),
                  pl.BlockSpec((kvh, ctx_len, HEAD_DIM), lambda b: (av, b, 0)),
                  pl.BlockSpec((kvh, 1, g * ctx_len), lambda b: (0, 0, 0))],
        out_specs=pl.BlockSpec((ctx_len, kvh * g * HEAD_DIM), lambda b: (b, 0)),
        out_shape=jax.ShapeDtypeStruct((batch * ctx_len, kvh * g * HEAD_DIM), BF16),
        compiler_params=_params("parallel"),
        name="context_attention",
    )(p_ctx, p_ctx, p_ctx, sink_row)


def _ret_body(lg_ref, q_ref, k_ref, v_ref, g_ref, *rest, n_lat, n_ctx, ctx_out):
    if ctx_out:
        cq_ref, ck_ref, cv_ref, cg_ref, gn_ref, o_ref, co_ref, yf_ref = rest
    else:
        ck_ref, cv_ref, gn_ref, o_ref, yf_ref = rest
        cq_ref = cg_ref = co_ref = None
    c = RET_CHUNK
    h = pl.program_id(1)
    lgf = lg_ref[0, h]
    lgb = lg_ref[1, h]
    ii = lax.broadcasted_iota(jnp.int32, (c, c), 0)
    jj = lax.broadcasted_iota(jnp.int32, (c, c), 1)
    rel = (ii - jj).astype(F32)
    dm_f = jnp.where(rel >= 0, jnp.exp(lgf * jnp.maximum(rel, 0.0)), 0.0)
    dm_b = jnp.where(rel < 0, jnp.exp(lgb * jnp.maximum(-rel, 0.0)), 0.0)
    pos = lax.broadcasted_iota(jnp.int32, (c, HEAD_DIM), 0).astype(F32)
    qd_f = jnp.exp(lgf * (pos + 1.0))
    kd_f = jnp.exp(lgf * (c - 1.0 - pos))
    qd_b = jnp.exp(lgb * (c - pos))
    kd_b = jnp.exp(lgb * pos)
    cd_f = jnp.exp(jnp.full((1, RET_V_DIM), lgf * c, F32))
    cd_b = jnp.exp(jnp.full((1, RET_V_DIM), lgb * c, F32))
    gn = gn_ref[0]

    def rows(ci):
        return slice(ci * c, (ci + 1) * c)

    def wide(ref, ci):
        return jnp.concatenate([ref[0, rows(ci), :], ref[1, rows(ci), :]], axis=1)

    def step(q, k, v, state, dm, qd, kd, cd):
        kv = lax.dot_general((k.astype(F32) * kd).astype(BF16), v, (((0,), (0,)), ((), ())),
                             preferred_element_type=F32)
        out = None
        if q is not None:
            s = lax.dot_general(q, k, (((1,), (1,)), ((), ())), preferred_element_type=F32) * dm
            out = jnp.dot(s.astype(BF16), v, preferred_element_type=F32)
            if state is not None:
                out = out + jnp.dot((q.astype(F32) * qd).astype(BF16), state.astype(BF16),
                                    preferred_element_type=F32)
        return out, (kv if state is None else state * cd + kv)

    def finish(y, gate):
        mu = jnp.mean(y, axis=-1, keepdims=True)
        dlt = y - mu
        var = jnp.mean(dlt * dlt, axis=-1, keepdims=True)
        yn = dlt * lax.rsqrt(var + EPS) * gn
        gf = gate.astype(F32)
        return ((gf * jax.nn.sigmoid(gf)) * yn).astype(BF16)

    n_y_ctx = n_ctx if ctx_out else 0

    state = None
    for ci in range(n_ctx):
        q = cq_ref[0, rows(ci), :] if ctx_out else None
        out, state = step(q, ck_ref[0, rows(ci), :], wide(cv_ref, ci), state, dm_f, qd_f, kd_f, cd_f)
        if ctx_out:
            yf_ref[rows(ci), :] = out
    for ci in range(n_lat):
        out, state = step(q_ref[0, rows(ci), :], k_ref[0, rows(ci), :], wide(v_ref, ci), state,
                          dm_f, qd_f, kd_f, cd_f)
        yf_ref[rows(n_y_ctx + ci), :] = out

    state = None
    for ci in reversed(range(n_ctx)):
        q = cq_ref[0, rows(ci), :] if ctx_out else None
        out, state = step(q, ck_ref[0, rows(ci), :], wide(cv_ref, ci), state, dm_b, qd_b, kd_b, cd_b)
        if ctx_out:
            co_ref[rows(ci), :] = finish(yf_ref[rows(ci), :] + out, wide(cg_ref, ci))
    for ci in reversed(range(n_lat)):
        out, state = step(q_ref[0, rows(ci), :], k_ref[0, rows(ci), :], wide(v_ref, ci), state,
                          dm_b, qd_b, kd_b, cd_b)
        o_ref[rows(ci), :] = finish(yf_ref[rows(n_y_ctx + ci), :] + out, wide(g_ref, ci))


def _retention(p_lat, p_ctx, log_decay, gn_g, layout, batch, seq_len, ctx_len, ctx_out):
    nh = layout.ret_heads
    rq, rk, rv, rg = layout.rq0, layout.rk0, layout.rv0 // 2, layout.rg0 // 2
    narrow = lambda base, ln: pl.BlockSpec((1, ln, HEAD_DIM), lambda b, h, lg: (base + h, b, 0))
    wide = lambda base, ln: pl.BlockSpec((2, ln, HEAD_DIM), lambda b, h, lg: (base + h, b, 0))
    in_specs = [narrow(rq, seq_len), narrow(rk, seq_len), wide(rv, seq_len), wide(rg, seq_len)]
    args = [p_lat] * 4
    if ctx_out:
        in_specs += [narrow(rq, ctx_len), narrow(rk, ctx_len), wide(rv, ctx_len), wide(rg, ctx_len)]
        args += [p_ctx] * 4
    else:
        in_specs += [narrow(rk, ctx_len), wide(rv, ctx_len)]
        args += [p_ctx] * 2
    in_specs.append(pl.BlockSpec((1, 1, RET_V_DIM), lambda b, h, lg: (h, 0, 0)))
    args.append(gn_g.reshape(nh, 1, RET_V_DIM))
    out_specs = [pl.BlockSpec((seq_len, RET_V_DIM), lambda b, h, lg: (b, h))]
    out_shape = [jax.ShapeDtypeStruct((batch * seq_len, nh * RET_V_DIM), BF16)]
    y_rows = seq_len
    if ctx_out:
        out_specs.append(pl.BlockSpec((ctx_len, RET_V_DIM), lambda b, h, lg: (b, h)))
        out_shape.append(jax.ShapeDtypeStruct((batch * ctx_len, nh * RET_V_DIM), BF16))
        y_rows += ctx_len
    outs = pl.pallas_call(
        functools.partial(_ret_body, n_lat=seq_len // RET_CHUNK, n_ctx=ctx_len // RET_CHUNK, ctx_out=ctx_out),
        grid_spec=pltpu.PrefetchScalarGridSpec(
            num_scalar_prefetch=1,
            grid=(batch, nh),
            in_specs=in_specs,
            out_specs=out_specs,
            scratch_shapes=[pltpu.VMEM((y_rows, RET_V_DIM), F32)]),
        out_shape=out_shape,
        compiler_params=_params("parallel", "parallel"),
        name="retention" if ctx_out else "retention_last",
    )(log_decay, *args)
    return (outs[0], outs[1]) if ctx_out else (outs[0], None)


def _residual_epilogue(xo_ref, x_ref, gt_ref, gpost_ref, nxt, part_ref, tm):
    d = xo_ref.shape[1]
    nc = d // LANES
    blk = (BF16_ROWS // SUBLANES, SUBLANES, LANES)
    n_blocks = tm // BF16_ROWS

    def rows_of(rb):
        return pl.ds(pl.multiple_of(rb * BF16_ROWS, BF16_ROWS), BF16_ROWS)

    def cols(c):
        return slice(c * LANES, (c + 1) * LANES)

    def partials_to_rsqrt():
        ss = jnp.sum(part_ref[...], axis=-1, keepdims=True)
        part_ref[...] = jnp.broadcast_to(lax.rsqrt(ss / d + EPS), part_ref.shape)

    def sweep_sumsq(rb, carry):
        rws = rows_of(rb)
        acc = [jnp.zeros((BF16_ROWS, LANES), F32), jnp.zeros((BF16_ROWS, LANES), F32)]
        for c in range(nc):
            y = xo_ref[rws, cols(c)]
            acc[c % 2] = acc[c % 2] + y * y
        part_ref[rws, :] = acc[0] + acc[1]
        return carry

    def sweep_residual(rb, carry):
        rws = rows_of(rb)
        r = part_ref[rws, :].reshape(blk)
        acc = [jnp.zeros(blk, F32), jnp.zeros(blk, F32)]
        for c in range(nc):
            y = xo_ref[rws, cols(c)].reshape(blk)
            xn = x_ref[rws, cols(c)].reshape(blk) + gt_ref[:, :, cols(c)] * ((y * r) * gpost_ref[:, cols(c)][None])
            xo_ref[rws, cols(c)] = xn.reshape(BF16_ROWS, LANES)
            acc[c % 2] = acc[c % 2] + xn * xn
        part_ref[rws, :] = (acc[0] + acc[1]).reshape(BF16_ROWS, LANES)
        return carry

    lax.fori_loop(0, n_blocks, sweep_sumsq, 0, unroll=2)
    partials_to_rsqrt()
    lax.fori_loop(0, n_blocks, sweep_residual, 0, unroll=2)
    if nxt is None:
        return
    gpre_ref, sh_ref, sc_ref, hn_ref = nxt
    sc1 = 1.0 + sc_ref[...]

    def sweep_prenorm(rb, carry):
        rws = rows_of(rb)
        r2 = part_ref[rws, :].reshape(blk)
        for c in range(nc):
            xn = xo_ref[rws, cols(c)].reshape(blk)
            hn = ((xn * r2) * gpre_ref[:, cols(c)][None]) * sc1[:, :, cols(c)] + sh_ref[:, :, cols(c)]
            hn_ref[rws, cols(c)] = hn.reshape(BF16_ROWS, LANES).astype(BF16)
        return carry

    partials_to_rsqrt()
    lax.fori_loop(0, n_blocks, sweep_prenorm, 0, unroll=2)


def _outproj_body(a_ref, r_ref, w_ref, x_ref, gt_ref, gpost_ref, gpre_ref, sh_ref, sc_ref,
                  xo_ref, hn_ref, part_ref, *, nj, tn, tm):
    j = pl.program_id(1)
    ka = a_ref.shape[1]
    y = (jnp.dot(a_ref[...], w_ref[0, 0, :ka, :], preferred_element_type=F32)
         + jnp.dot(r_ref[...], w_ref[0, 0, ka:, :], preferred_element_type=F32))
    for jj in range(nj):
        @pl.when(j == jj)
        def _():
            xo_ref[:, jj * tn:(jj + 1) * tn] = y

    @pl.when(j == nj - 1)
    def _():
        _residual_epilogue(xo_ref, x_ref, gt_ref, gpost_ref, (gpre_ref, sh_ref, sc_ref, hn_ref), part_ref, tm)


def _resident(shape, index_map):
    return pl.BlockSpec(shape, index_map, pipeline_mode=pl.Buffered(1))


def _outproj(att, ret, w_out, li, x, gt, g_post, g_pre, sh, sc, group_of_tile, tm=512):
    m, d = x.shape
    ka = att.shape[1]
    nj, tn = w_out.shape[1], w_out.shape[3]
    assert ret.shape[1] == ka and w_out.shape[2] == 2 * ka and nj * tn == d
    mod_spec = pl.BlockSpec((1, SUBLANES, d), lambda i, j: (group_of_tile(i, tm), 0, 0))
    vec_spec = pl.BlockSpec((SUBLANES, d), lambda i, j: (0, 0))
    row_tile = lambda i, j: (i, 0)
    return pl.pallas_call(
        functools.partial(_outproj_body, nj=nj, tn=tn, tm=tm),
        grid=(m // tm, nj),
        in_specs=[pl.BlockSpec((tm, ka), row_tile), pl.BlockSpec((tm, ka), row_tile),
                  pl.BlockSpec((1, 1, 2 * ka, tn), lambda i, j: (li, j, 0, 0)),
                  _resident((tm, d), row_tile), mod_spec, vec_spec, vec_spec, mod_spec, mod_spec],
        out_specs=[pl.BlockSpec((tm, d), row_tile), pl.BlockSpec((tm, d), row_tile)],
        out_shape=[jax.ShapeDtypeStruct((m, d), F32), jax.ShapeDtypeStruct((m, d), BF16)],
        scratch_shapes=[pltpu.VMEM((tm, LANES), F32)],
        compiler_params=_params("parallel", "arbitrary"),
        name="outproj",
    )(att, ret, w_out, x, gt, _rep8(g_post), _rep8(g_pre), sh, sc)


def _mlp_body(h_ref, wu_ref, wd_ref, x_ref, gt_ref, gpost_ref, *rest, nf, nt, tm, has_next):
    if has_next:
        gpre_ref, sh_ref, sc_ref, xo_ref, hn_ref, part_ref = rest
        nxt = (gpre_ref, sh_ref, sc_ref, hn_ref)
    else:
        xo_ref, part_ref = rest
        nxt = None
    f = pl.program_id(1)

    @pl.when(f == 0)
    def _():
        xo_ref[...] = jnp.zeros_like(xo_ref)

    u = jnp.dot(h_ref[...], wu_ref[0, 0], preferred_element_type=F32)
    a = jnp.maximum(u, 0.0)
    a = (a * a).astype(BF16)
    for nn in range(xo_ref.shape[1] // nt):
        cols = slice(nn * nt, (nn + 1) * nt)
        xo_ref[:, cols] += jnp.dot(a, wd_ref[0, :, cols], preferred_element_type=F32)

    @pl.when(f == nf - 1)
    def _():
        _residual_epilogue(xo_ref, x_ref, gt_ref, gpost_ref, nxt, part_ref, tm)


def _mlp(h, w_up, w_down, li, x, gt, g_post, nxt, group_of_tile, tm=512, nt=512):
    m, d = x.shape
    nf, tf = w_up.shape[1], w_up.shape[3]
    assert w_down.shape[1] == nf * tf
    mod_spec = pl.BlockSpec((1, SUBLANES, d), lambda i, f: (group_of_tile(i, tm), 0, 0))
    vec_spec = pl.BlockSpec((SUBLANES, d), lambda i, f: (0, 0))
    row_tile = lambda i, f: (i, 0)
    in_specs = [_resident((tm, d), row_tile),
                pl.BlockSpec((1, 1, d, tf), lambda i, f: (li, f, 0, 0)),
                pl.BlockSpec((1, tf, d), lambda i, f: (li, f, 0)),
                _resident((tm, d), row_tile), mod_spec, vec_spec]
    args = [h, w_up, w_down, x, gt, _rep8(g_post)]
    out_specs = [pl.BlockSpec((tm, d), row_tile)]
    out_shape = [jax.ShapeDtypeStruct((m, d), F32)]
    if nxt is not None:
        g_pre, sh, sc = nxt
        in_specs += [vec_spec, mod_spec, mod_spec]
        args += [_rep8(g_pre), sh, sc]
        out_specs.append(pl.BlockSpec((tm, d), row_tile))
        out_shape.append(jax.ShapeDtypeStruct((m, d), BF16))
    outs = pl.pallas_call(
        functools.partial(_mlp_body, nf=nf, nt=nt, tm=tm, has_next=nxt is not None),
        grid=(m // tm, nf),
        in_specs=in_specs,
        out_specs=out_specs,
        out_shape=out_shape,
        scratch_shapes=[pltpu.VMEM((tm, LANES), F32)],
        compiler_params=_params("parallel", "arbitrary"),
        name="mlp" if nxt is not None else "mlp_last",
    )(*args)
    return (outs[0], outs[1]) if nxt is not None else (outs[0], None)


class _Layout:
    def __init__(self, d_model):
        self.attn_heads = d_model // (2 * HEAD_DIM)
        self.kv_heads = self.attn_heads // ATTN_GROUP
        self.ret_heads = d_model // (2 * RET_V_DIM)
        vchunks = RET_V_DIM // LANES
        self.aq0 = 0
        self.ak0 = self.aq0 + self.attn_heads
        self.av0 = self.ak0 + self.kv_heads
        self.rq0 = self.av0 + self.kv_heads
        self.rk0 = self.rq0 + self.ret_heads
        self.rv0 = self.rk0 + self.ret_heads
        self.rg0 = self.rv0 + vchunks * self.ret_heads
        self.n_chunks = self.rg0 + vchunks * self.ret_heads

    def kind_of_chunk(self, c):
        if c < self.ak0:
            return "q"
        if c < self.av0:
            return "rope"
        if c < self.rq0:
            return "none"
        if c < self.rk0:
            return "rope"
        if c < self.rv0:
            return "rk"
        return "none"


def _rope_tables(seq_len):
    pairs = HEAD_DIM // 4
    t = jnp.arange(seq_len)
    r = (t // GRID_W).astype(F32)
    cl = (t % GRID_W).astype(F32)
    inv = ROPE_BASE ** (-jnp.arange(pairs, dtype=F32) / pairs)
    ar = r[:, None] * inv
    ac = cl[:, None] * inv
    ang = jnp.concatenate([ar, ar, ac, ac], axis=-1)
    cos, sin = jnp.cos(ang), jnp.sin(ang)
    first = (jnp.arange(HEAD_DIM) % (2 * pairs)) < pairs
    return cos, jnp.where(first, -sin, 0.0), jnp.where(first, 0.0, sin)


def kernel(x, c, ctx, c_ctx, w_ada, b_ada, norm_g, w_in, attn_sink, ret_log_decay, ret_gn_g, w_out, w_up, w_down):
    batch, seq_len, d = x.shape
    ctx_len = ctx.shape[1]
    depth = w_in.shape[0]
    layout = _Layout(d)
    assert layout.n_chunks * LANES == w_in.shape[2]
    assert batch + 1 <= MOD_ROWS

    xf = x.reshape(batch * seq_len, d)
    cf = ctx.reshape(batch * ctx_len, d)

    c_all = jnp.concatenate([c, c_ctx[None, :], jnp.zeros((MOD_ROWS - batch - 1, d), F32)], axis=0)
    mod = _adaln(c_all, w_ada, b_ada)
    mod = mod.reshape(depth, MOD_ROWS, 6, 1, d).transpose(0, 2, 1, 3, 4)
    mod = jnp.broadcast_to(mod, (depth, 6, MOD_ROWS, SUBLANES, d))
    lat_group = lambda i, tm: (i * tm) // seq_len
    ctx_group = lambda i, tm: batch

    rope_tabs = _rope_tables(seq_len)
    g = ATTN_GROUP
    wi = _column_tiles(w_in, IN_TN)
    wo = _column_tiles(w_out, OUT_TN)
    wu = _column_tiles(w_up, FF_TN)
    wd = w_down.astype(BF16)

    h = _prenorm(xf, norm_g[0, 0], mod[0, 0], mod[0, 1], lat_group)
    hc = _prenorm(cf, norm_g[0, 0], mod[0, 0], mod[0, 1], ctx_group)
    for li in range(depth):
        last = li == depth - 1
        sh1, sc1, gt1, sh2, sc2, gt2 = [mod[li, k] for k in range(6)]
        g_pre_mix, g_post_mix, g_pre_mlp, g_post_mlp = [norm_g[li, k] for k in range(4)]
        sink = attn_sink[li].reshape(layout.kv_heads, g)

        p_lat = _inproj(h, wi, li, layout, rope_tabs, seq_len)
        p_ctx = _inproj(hc, wi, li, layout, None, ctx_len)
        sink_lat = jnp.repeat(sink, ATTN_BLOCK, axis=1)[:, None, :]
        att = _attention(p_lat, p_ctx, sink_lat, layout, batch, seq_len, ctx_len)
        ret, cret = _retention(p_lat, p_ctx, ret_log_decay[li], ret_gn_g[li], layout,
                               batch, seq_len, ctx_len, ctx_out=not last)
        xf, h2 = _outproj(att, ret, wo, li, xf, gt1, g_post_mix, g_pre_mlp, sh2, sc2, lat_group)
        nxt = None if last else (norm_g[li + 1, 0], mod[li + 1, 0], mod[li + 1, 1])
        xf, h = _mlp(h2, wu, wd, li, xf, gt2, g_post_mlp, nxt, lat_group)
        if not last:
            sink_ctx = jnp.repeat(sink, ctx_len, axis=1)[:, None, :]
            catt = _ctx_attention(p_ctx, sink_ctx, layout, batch, ctx_len)
            cf, hc2 = _outproj(catt, cret, wo, li, cf, gt1, g_post_mix, g_pre_mlp, sh2, sc2, ctx_group)
            cf, hc = _mlp(hc2, wu, wd, li, cf, gt2, g_post_mlp, nxt, ctx_group)
    return xf.reshape(batch, seq_len, d)
```

```python
import functools

import jax
import jax.numpy as jnp
from jax import lax
from jax.experimental import pallas as pl
from jax.experimental.pallas import tpu as pltpu

F32 = jnp.float32
BF16 = jnp.bfloat16

HEAD_DIM = 128
GRID_W = 64
WINDOW = 128
ATTN_BLOCK = 128
ATTN_GROUP = 4
RET_CHUNK = 128
RET_V_DIM = 2 * HEAD_DIM
ROPE_BASE = 10000.0
EPS = 1e-6
MASKED = -1e30

LANES = 128
SUBLANES = 8
BF16_ROWS = 16
MOD_ROWS = 16
VMEM_LIMIT = 56 * 1024 * 1024
IN_TN = 512


def _rep8(v):
    return jnp.broadcast_to(v.reshape(1, -1), (SUBLANES, v.shape[-1]))


def _params(*sem):
    return pltpu.CompilerParams(dimension_semantics=sem, vmem_limit_bytes=VMEM_LIMIT)


def _adaln_body(c_ref, w_ref, b_ref, o_ref):
    c = c_ref[...]
    a = (c * jax.nn.sigmoid(c)).astype(BF16)
    o_ref[0] = jnp.dot(a, w_ref[0].astype(BF16), preferred_element_type=F32) + b_ref[0]


def _adaln(c_all, w_ada, b_ada, tn=512):
    depth, d, n = w_ada.shape
    rows = c_all.shape[0]
    return pl.pallas_call(
        _adaln_body,
        grid=(depth, n // tn),
        in_specs=[pl.BlockSpec((rows, d), lambda l, j: (0, 0)),
                  pl.BlockSpec((1, d, tn), lambda l, j: (l, 0, j)),
                  pl.BlockSpec((1, 1, tn), lambda l, j: (l, 0, j))],
        out_specs=pl.BlockSpec((1, rows, tn), lambda l, j: (l, 0, j)),
        out_shape=jax.ShapeDtypeStruct((depth, rows, n), F32),
        compiler_params=_params("parallel", "parallel"),
        name="adaln",
    )(c_all, w_ada, b_ada.reshape(depth, 1, n))


def _prenorm_body(x_ref, g_ref, sh_ref, sc_ref, o_ref):
    tm, d = x_ref.shape
    x = x_ref[...].reshape(tm // SUBLANES, SUBLANES, d)
    r = lax.rsqrt(jnp.mean(x * x, axis=-1, keepdims=True) + EPS)
    h = ((x * r) * g_ref[...][None]) * (1.0 + sc_ref[...]) + sh_ref[...]
    o_ref[...] = h.reshape(tm, d).astype(BF16)


def _prenorm(x, g, sh, sc, group_of_tile, tm=256):
    m, d = x.shape
    mod_spec = pl.BlockSpec((1, SUBLANES, d), lambda i: (group_of_tile(i, tm), 0, 0))
    return pl.pallas_call(
        _prenorm_body,
        grid=(m // tm,),
        in_specs=[pl.BlockSpec((tm, d), lambda i: (i, 0)),
                  pl.BlockSpec((SUBLANES, d), lambda i: (0, 0)),
                  mod_spec, mod_spec],
        out_specs=pl.BlockSpec((tm, d), lambda i: (i, 0)),
        out_shape=jax.ShapeDtypeStruct((m, d), BF16),
        compiler_params=_params("parallel"),
        name="prenorm",
    )(x, _rep8(g), sh, sc)


INPROJ_KINDS = ("q", "rope", "rk", "none")


def _inproj_body(kind_ref, tile_ref, h_ref, w_ref, c_ref, a_ref, b_ref, o_ref, acc_ref, *, n_steps, n_chunk):
    del kind_ref, tile_ref
    s = pl.program_id(0)
    slot = s % 2

    def multiply():
        acc_ref[slot] = jnp.dot(h_ref[...], w_ref[0], preferred_element_type=F32)

    def finish_previous():
        c, a, b = c_ref[0], a_ref[0], b_ref[0]
        for cc in range(n_chunk):
            t = acc_ref[1 - slot, :, cc * LANES:(cc + 1) * LANES]
            o_ref[cc] = (t * c + (pltpu.roll(t, 96, 1) * a + pltpu.roll(t, 32, 1) * b)).astype(BF16)

    @pl.when(s == 0)
    def _():
        multiply()

    @pl.when(jnp.logical_and(s > 0, s < n_steps))
    def _():
        finish_previous()
        multiply()

    @pl.when(s == n_steps)
    def _():
        finish_previous()


def _inproj_tables(rope_tabs, rows):
    if rope_tabs is None:
        cos = jnp.ones((rows, HEAD_DIM), F32)
        sa = sb = jnp.zeros((rows, HEAD_DIM), F32)
    else:
        cos, sa, sb = rope_tabs
    scale = {"q": HEAD_DIM ** -0.5, "rope": 1.0, "rk": HEAD_DIM ** -0.5}
    one, zero = jnp.ones_like(cos), jnp.zeros_like(cos)
    stack = lambda tab, ident: jnp.stack([ident if k == "none" else tab * scale[k] for k in INPROJ_KINDS])
    return stack(cos, one), stack(sa, zero), stack(sb, zero)


def _inproj(h, w, li, layout, rope_tabs, seq_len, tiles=None, tm=1024):
    m, d = h.shape
    tn = IN_TN
    n_chunk = tn // LANES
    tiles = tuple(range(w.shape[2] // tn)) if tiles is None else tuple(tiles)
    nj = len(tiles)
    kinds = []
    for t in tiles:
        ks = {layout.kind_of_chunk(c) for c in range(t * n_chunk, (t + 1) * n_chunk)}
        assert len(ks) == 1, "column tile must not straddle head groups"
        kinds.append(INPROJ_KINDS.index(ks.pop()))
    tab_rows = seq_len if rope_tabs is not None else tm
    per_seq = tab_rows // tm
    tabs = _inproj_tables(rope_tabs, tab_rows)
    n_steps = (m // tm) * nj
    cur = lambda s: jnp.minimum(s, n_steps - 1)
    prv = lambda s: jnp.maximum(s - 1, 0)
    tab_spec = pl.BlockSpec((1, tm, LANES),
                            lambda s, kind, tile: (kind[prv(s) % nj], (prv(s) // nj) % per_seq, 0))
    return pl.pallas_call(
        functools.partial(_inproj_body, n_steps=n_steps, n_chunk=n_chunk),
        grid_spec=pltpu.PrefetchScalarGridSpec(
            num_scalar_prefetch=2,
            grid=(n_steps + 1,),
            in_specs=[pl.BlockSpec((tm, d), lambda s, kind, tile: (cur(s) // nj, 0)),
                      pl.BlockSpec((1, d, tn), lambda s, kind, tile: (li, 0, tile[cur(s) % nj])),
                      tab_spec, tab_spec, tab_spec],
            out_specs=pl.BlockSpec((n_chunk, tm, LANES), lambda s, kind, tile: (prv(s) % nj, prv(s) // nj, 0)),
            scratch_shapes=[pltpu.VMEM((2, tm, tn), F32)]),
        out_shape=jax.ShapeDtypeStruct((nj * n_chunk, m, LANES), BF16),
        compiler_params=_params("arbitrary"),
        name="inproj_rope" if rope_tabs is not None else "inproj_ctx",
    )(jnp.asarray(kinds, jnp.int32), jnp.asarray(tiles, jnp.int32), h, w, *tabs)


def _attn_body(*refs, local, tq, kvh):
    if local:
        q_ref, kp_ref, kc_ref, kn_ref, vp_ref, vc_ref, vn_ref, ck_ref, cv_ref, sink_ref, bias_ref, o_ref = refs
    else:
        q_ref, ck_ref, cv_ref, sink_ref, o_ref = refs
    g = q_ref.shape[0] // kvh
    rows = g * tq
    nt = (((1,), (1,)), ((), ()))
    tn = (((0,), (0,)), ((), ()))
    for h in range(kvh):
        q = q_ref[h * g:(h + 1) * g].reshape(rows, HEAD_DIM)
        sink = sink_ref[h]
        s_ctx = lax.dot_general(ck_ref[h], q, nt, preferred_element_type=F32)
        m = jnp.maximum(jnp.max(s_ctx, axis=0, keepdims=True), sink)
        if local:
            k_loc = jnp.concatenate([kp_ref[h], kc_ref[h], kn_ref[h]], axis=0)
            s_loc = lax.dot_general(k_loc, q, nt, preferred_element_type=F32) + bias_ref[0]
            m = jnp.maximum(m, jnp.max(s_loc, axis=0, keepdims=True))
        p_ctx = jnp.exp(s_ctx - m)
        denom = jnp.sum(p_ctx, axis=0, keepdims=True) + jnp.exp(sink - m)
        ot = lax.dot_general(cv_ref[h], p_ctx.astype(BF16), tn, preferred_element_type=F32)
        if local:
            p_loc = jnp.exp(s_loc - m)
            denom = denom + jnp.sum(p_loc, axis=0, keepdims=True)
            v_loc = jnp.concatenate([vp_ref[h], vc_ref[h], vn_ref[h]], axis=0)
            ot = ot + lax.dot_general(v_loc, p_loc.astype(BF16), tn, preferred_element_type=F32)
        o = (ot / denom).T
        for gi in range(g):
            col = (h * g + gi) * HEAD_DIM
            o_ref[:, col:col + HEAD_DIM] = o[gi * tq:(gi + 1) * tq].astype(BF16)


def _band_bias(tq, n_qblocks, group):
    assert n_qblocks >= 2
    qq = jnp.tile(jnp.arange(tq), group)[None, :]
    cc = jnp.arange(3 * tq)[:, None]
    off = cc - qq
    band = (off >= 0) & (off <= 2 * WINDOW)
    first = band & (cc >= tq)
    last = band & (cc < 2 * tq)
    return jnp.where(jnp.stack([first, band, last]), 0.0, MASKED).astype(F32)


def _attention(p_lat, p_ctx, sink_row, layout, ctx_layout, batch, seq_len, ctx_len):
    g, tq = ATTN_GROUP, ATTN_BLOCK
    nq = seq_len // tq
    kvh = layout.kv_heads
    assert layout.aq0 == 0 and all(c % kvh == 0 for c in (layout.ak0, layout.av0, ctx_layout.ak0, ctx_layout.av0))
    ak, av = layout.ak0 // kvh, layout.av0 // kvh
    cak, cav = ctx_layout.ak0 // kvh, ctx_layout.av0 // kvh
    prev = lambda b, n: b * nq + jnp.maximum(n - 1, 0)
    cur = lambda b, n: b * nq + n
    nxt = lambda b, n: b * nq + jnp.minimum(n + 1, nq - 1)
    in_specs = [pl.BlockSpec((kvh * g, tq, HEAD_DIM), lambda b, n: (0, cur(b, n), 0))]
    for base in (ak, av):
        for row in (prev, cur, nxt):
            in_specs.append(pl.BlockSpec((kvh, tq, HEAD_DIM), functools.partial(
                lambda b, n, base, row: (base, row(b, n), 0), base=base, row=row)))
    edge = lambda b, n: jnp.where(n == 0, 0, jnp.where(n == nq - 1, 2, 1))
    in_specs += [pl.BlockSpec((kvh, ctx_len, HEAD_DIM), lambda b, n: (cak, b, 0)),
                 pl.BlockSpec((kvh, ctx_len, HEAD_DIM), lambda b, n: (cav, b, 0)),
                 pl.BlockSpec((kvh, 1, g * tq), lambda b, n: (0, 0, 0)),
                 pl.BlockSpec((1, 3 * tq, g * tq), lambda b, n: (edge(b, n), 0, 0))]
    return pl.pallas_call(
        functools.partial(_attn_body, local=True, tq=tq, kvh=kvh),
        grid=(batch, nq),
        in_specs=in_specs,
        out_specs=pl.BlockSpec((tq, kvh * g * HEAD_DIM), lambda b, n: (cur(b, n), 0)),
        out_shape=jax.ShapeDtypeStruct((batch * seq_len, kvh * g * HEAD_DIM), BF16),
        compiler_params=_params("parallel", "parallel"),
        name="window_attention",
    )(p_lat, *([p_lat] * 6), p_ctx, p_ctx, sink_row, _band_bias(tq, nq, g))


def _ctx_attention(p_ctx, sink_row, layout, batch, ctx_len):
    g = ATTN_GROUP
    kvh = layout.kv_heads
    ak, av = layout.ak0 // kvh, layout.av0 // kvh
    return pl.pallas_call(
        functools.partial(_attn_body, local=False, tq=ctx_len, kvh=kvh),
        grid=(batch,),
        in_specs=[pl.BlockSpec((kvh * g, ctx_len, HEAD_DIM), lambda b: (0, b, 0)),
                  pl.BlockSpec((kvh, ctx_len, HEAD_DIM), lambda b: (ak, b, 0)),
                  pl.BlockSpec((kvh, ctx_len, HEAD_DIM), lambda b: (av, b, 0)),
                  pl.BlockSpec((kvh, 1, g * ctx_len), lambda b: (0, 0, 0))],
        out_specs=pl.BlockSpec((ctx_len, kvh * g * HEAD_DIM), lambda b: (b, 0)),
        out_shape=jax.ShapeDtypeStruct((batch * ctx_len, kvh * g * HEAD_DIM), BF16),
        compiler_params=_params("parallel"),
        name="context_attention",
    )(p_ctx, p_ctx, p_ctx, sink_row)


def _ret_body(lg_ref, q_ref, k_ref, v_ref, g_ref, *rest, n_lat, n_ctx, ctx_out):
    if ctx_out:
        cq_ref, ck_ref, cv_ref, cg_ref, gn_ref, o_ref, co_ref, yf_ref = rest
    else:
        ck_ref, cv_ref, gn_ref, o_ref, yf_ref = rest
        cq_ref = cg_ref = co_ref = None
    c = RET_CHUNK
    h = pl.program_id(1)
    lgf = lg_ref[0, h]
    lgb = lg_ref[1, h]
    ii = lax.broadcasted_iota(jnp.int32, (c, c), 0)
    jj = lax.broadcasted_iota(jnp.int32, (c, c), 1)
    rel = (ii - jj).astype(F32)
    dm_f = jnp.where(rel >= 0, jnp.exp(lgf * jnp.maximum(rel, 0.0)), 0.0)
    dm_b = jnp.where(rel < 0, jnp.exp(lgb * jnp.maximum(-rel, 0.0)), 0.0)
    pos = lax.broadcasted_iota(jnp.int32, (c, HEAD_DIM), 0).astype(F32)
    qd_f = jnp.exp(lgf * (pos + 1.0))
    kd_f = jnp.exp(lgf * (c - 1.0 - pos))
    qd_b = jnp.exp(lgb * (c - pos))
    kd_b = jnp.exp(lgb * pos)
    cd_f = jnp.exp(jnp.full((1, RET_V_DIM), lgf * c, F32))
    cd_b = jnp.exp(jnp.full((1, RET_V_DIM), lgb * c, F32))
    gn = gn_ref[0]

    def rows(ci):
        return slice(ci * c, (ci + 1) * c)

    def wide(ref, ci):
        return jnp.concatenate([ref[0, rows(ci), :], ref[1, rows(ci), :]], axis=1)

    def step(q, k, v, state, dm, qd, kd, cd):
        kv = lax.dot_general((k.astype(F32) * kd).astype(BF16), v, (((0,), (0,)), ((), ())),
                             preferred_element_type=F32)
        out = None
        if q is not None:
            s = lax.dot_general(q, k, (((1,), (1,)), ((), ())), preferred_element_type=F32) * dm
            out = jnp.dot(s.astype(BF16), v, preferred_element_type=F32)
            if state is not None:
                out = out + jnp.dot((q.astype(F32) * qd).astype(BF16), state.astype(BF16),
                                    preferred_element_type=F32)
        return out, (kv if state is None else state * cd + kv)

    def finish(y, gate):
        mu = jnp.mean(y, axis=-1, keepdims=True)
        dlt = y - mu
        var = jnp.mean(dlt * dlt, axis=-1, keepdims=True)
        yn = dlt * lax.rsqrt(var + EPS) * gn
        gf = gate.astype(F32)
        return ((gf * jax.nn.sigmoid(gf)) * yn).astype(BF16)

    n_y_ctx = n_ctx if ctx_out else 0

    state = None
    for ci in range(n_ctx):
        q = cq_ref[0, rows(ci), :] if ctx_out else None
        out, state = step(q, ck_ref[0, rows(ci), :], wide(cv_ref, ci), state, dm_f, qd_f, kd_f, cd_f)
        if ctx_out:
            yf_ref[rows(ci), :] = out
    for ci in range(n_lat):
        out, state = step(q_ref[0, rows(ci), :], k_ref[0, rows(ci), :], wide(v_ref, ci), state,
                          dm_f, qd_f, kd_f, cd_f)
        yf_ref[rows(n_y_ctx + ci), :] = out

    state = None
    for ci in reversed(range(n_ctx)):
        q = cq_ref[0, rows(ci), :] if ctx_out else None
        out, state = step(q, ck_ref[0, rows(ci), :], wide(cv_ref, ci), state, dm_b, qd_b, kd_b, cd_b)
        if ctx_out:
            co_ref[rows(ci), :] = finish(yf_ref[rows(ci), :] + out, wide(cg_ref, ci))
    for ci in reversed(range(n_lat)):
        out, state = step(q_ref[0, rows(ci), :], k_ref[0, rows(ci), :], wide(v_ref, ci), state,
                          dm_b, qd_b, kd_b, cd_b)
        o_ref[rows(ci), :] = finish(yf_ref[rows(n_y_ctx + ci), :] + out, wide(g_ref, ci))


def _retention(p_lat, p_ctx, log_decay, gn_g, layout, ctx_layout, batch, seq_len, ctx_len, ctx_out):
    nh = layout.ret_heads
    vchunks = RET_V_DIM // LANES
    narrow = lambda base, ln: pl.BlockSpec((1, ln, HEAD_DIM), lambda b, h, lg: (base + h, b, 0))
    wide = lambda base, ln: pl.BlockSpec((vchunks, ln, HEAD_DIM), lambda b, h, lg: (base // vchunks + h, b, 0))
    in_specs = [narrow(layout.rq0, seq_len), narrow(layout.rk0, seq_len),
                wide(layout.rv0, seq_len), wide(layout.rg0, seq_len)]
    args = [p_lat] * 4
    if ctx_out:
        in_specs += [narrow(ctx_layout.rq0, ctx_len), narrow(ctx_layout.rk0, ctx_len),
                     wide(ctx_layout.rv0, ctx_len), wide(ctx_layout.rg0, ctx_len)]
        args += [p_ctx] * 4
    else:
        in_specs += [narrow(ctx_layout.rk0, ctx_len), wide(ctx_layout.rv0, ctx_len)]
        args += [p_ctx] * 2
    in_specs.append(pl.BlockSpec((1, 1, RET_V_DIM), lambda b, h, lg: (h, 0, 0)))
    args.append(gn_g.reshape(nh, 1, RET_V_DIM))
    out_specs = [pl.BlockSpec((seq_len, RET_V_DIM), lambda b, h, lg: (b, h))]
    out_shape = [jax.ShapeDtypeStruct((batch * seq_len, nh * RET_V_DIM), BF16)]
    y_rows = seq_len
    if ctx_out:
        out_specs.append(pl.BlockSpec((ctx_len, RET_V_DIM), lambda b, h, lg: (b, h)))
        out_shape.append(jax.ShapeDtypeStruct((batch * ctx_len, nh * RET_V_DIM), BF16))
        y_rows += ctx_len
    outs = pl.pallas_call(
        functools.partial(_ret_body, n_lat=seq_len // RET_CHUNK, n_ctx=ctx_len // RET_CHUNK, ctx_out=ctx_out),
        grid_spec=pltpu.PrefetchScalarGridSpec(
            num_scalar_prefetch=1,
            grid=(batch, nh),
            in_specs=in_specs,
            out_specs=out_specs,
            scratch_shapes=[pltpu.VMEM((y_rows, RET_V_DIM), F32)]),
        out_shape=out_shape,
        compiler_params=_params("parallel", "parallel"),
        name="retention" if ctx_out else "retention_last",
    )(log_decay, *args)
    return (outs[0], outs[1]) if ctx_out else (outs[0], None)


def _residual_epilogue(xo_ref, x_ref, gt_ref, gpost_ref, nxt, part_ref, tm):
    d = xo_ref.shape[1]
    nc = d // LANES
    blk = (BF16_ROWS // SUBLANES, SUBLANES, LANES)
    n_blocks = tm // BF16_ROWS

    def rows_of(rb):
        return pl.ds(pl.multiple_of(rb * BF16_ROWS, BF16_ROWS), BF16_ROWS)

    def cols(c):
        return slice(c * LANES, (c + 1) * LANES)

    def partials_to_rsqrt():
        ss = jnp.sum(part_ref[...], axis=-1, keepdims=True)
        part_ref[...] = jnp.broadcast_to(lax.rsqrt(ss / d + EPS), part_ref.shape)

    def sweep_sumsq(rb, carry):
        rws = rows_of(rb)
        acc = [jnp.zeros((BF16_ROWS, LANES), F32), jnp.zeros((BF16_ROWS, LANES), F32)]
        for c in range(nc):
            y = xo_ref[rws, cols(c)]
            acc[c % 2] = acc[c % 2] + y * y
        part_ref[rws, :] = acc[0] + acc[1]
        return carry

    def sweep_residual(rb, carry):
        rws = rows_of(rb)
        r = part_ref[rws, :].reshape(blk)
        acc = [jnp.zeros(blk, F32), jnp.zeros(blk, F32)]
        for c in range(nc):
            y = xo_ref[rws, cols(c)].reshape(blk)
            xn = x_ref[rws, cols(c)].reshape(blk) + gt_ref[:, :, cols(c)] * ((y * r) * gpost_ref[:, cols(c)][None])
            xo_ref[rws, cols(c)] = xn.reshape(BF16_ROWS, LANES)
            acc[c % 2] = acc[c % 2] + xn * xn
        part_ref[rws, :] = (acc[0] + acc[1]).reshape(BF16_ROWS, LANES)
        return carry

    lax.fori_loop(0, n_blocks, sweep_sumsq, 0, unroll=2)
    partials_to_rsqrt()
    lax.fori_loop(0, n_blocks, sweep_residual, 0, unroll=2)
    if nxt is None:
        return
    gpre_ref, sh_ref, sc_ref, hn_ref = nxt
    sc1 = 1.0 + sc_ref[...]

    def sweep_prenorm(rb, carry):
        rws = rows_of(rb)
        r2 = part_ref[rws, :].reshape(blk)
        for c in range(nc):
            xn = xo_ref[rws, cols(c)].reshape(blk)
            hn = ((xn * r2) * gpre_ref[:, cols(c)][None]) * sc1[:, :, cols(c)] + sh_ref[:, :, cols(c)]
            hn_ref[rws, cols(c)] = hn.reshape(BF16_ROWS, LANES).astype(BF16)
        return carry

    partials_to_rsqrt()
    lax.fori_loop(0, n_blocks, sweep_prenorm, 0, unroll=2)


def _outproj_body(a_ref, r_ref, wa_ref, wb_ref, x_ref, gt_ref, gpost_ref, gpre_ref, sh_ref, sc_ref,
                  xo_ref, hn_ref, part_ref, *, nj, tn, tm):
    j = pl.program_id(1)
    y = (jnp.dot(a_ref[...], wa_ref[0], preferred_element_type=F32)
         + jnp.dot(r_ref[...], wb_ref[0], preferred_element_type=F32))
    for jj in range(nj):
        @pl.when(j == jj)
        def _():
            xo_ref[:, jj * tn:(jj + 1) * tn] = y

    @pl.when(j == nj - 1)
    def _():
        _residual_epilogue(xo_ref, x_ref, gt_ref, gpost_ref, (gpre_ref, sh_ref, sc_ref, hn_ref), part_ref, tm)


def _resident(shape, index_map):
    return pl.BlockSpec(shape, index_map, pipeline_mode=pl.Buffered(1))


def _outproj(att, ret, w_out, li, x, gt, g_post, g_pre, sh, sc, group_of_tile, tm=512, tn=512):
    m, d = x.shape
    ka = att.shape[1]
    assert ret.shape[1] == ka and w_out.shape[1] == 2 * ka
    nj = d // tn
    mod_spec = pl.BlockSpec((1, SUBLANES, d), lambda i, j: (group_of_tile(i, tm), 0, 0))
    vec_spec = pl.BlockSpec((SUBLANES, d), lambda i, j: (0, 0))
    row_tile = lambda i, j: (i, 0)
    return pl.pallas_call(
        functools.partial(_outproj_body, nj=nj, tn=tn, tm=tm),
        grid=(m // tm, nj),
        in_specs=[pl.BlockSpec((tm, ka), row_tile), pl.BlockSpec((tm, ka), row_tile),
                  pl.BlockSpec((1, ka, tn), lambda i, j: (li, 0, j)),
                  pl.BlockSpec((1, ka, tn), lambda i, j: (li, 1, j)),
                  _resident((tm, d), row_tile), mod_spec, vec_spec, vec_spec, mod_spec, mod_spec],
        out_specs=[pl.BlockSpec((tm, d), row_tile), pl.BlockSpec((tm, d), row_tile)],
        out_shape=[jax.ShapeDtypeStruct((m, d), F32), jax.ShapeDtypeStruct((m, d), BF16)],
        scratch_shapes=[pltpu.VMEM((tm, LANES), F32)],
        compiler_params=_params("parallel", "arbitrary"),
        name="outproj",
    )(att, ret, w_out, w_out, x, gt, _rep8(g_post), _rep8(g_pre), sh, sc)


def _mlp_body(h_ref, wu_ref, wd_ref, x_ref, gt_ref, gpost_ref, *rest, nf, nt, tm, has_next):
    if has_next:
        gpre_ref, sh_ref, sc_ref, xo_ref, hn_ref, part_ref = rest
        nxt = (gpre_ref, sh_ref, sc_ref, hn_ref)
    else:
        xo_ref, part_ref = rest
        nxt = None
    f = pl.program_id(1)

    @pl.when(f == 0)
    def _():
        xo_ref[...] = jnp.zeros_like(xo_ref)

    u = jnp.dot(h_ref[...], wu_ref[0], preferred_element_type=F32)
    a = jnp.maximum(u, 0.0)
    a = (a * a).astype(BF16)
    for nn in range(xo_ref.shape[1] // nt):
        cols = slice(nn * nt, (nn + 1) * nt)
        xo_ref[:, cols] += jnp.dot(a, wd_ref[0, :, cols], preferred_element_type=F32)

    @pl.when(f == nf - 1)
    def _():
        _residual_epilogue(xo_ref, x_ref, gt_ref, gpost_ref, nxt, part_ref, tm)


def _mlp(h, w_up, w_down, li, x, gt, g_post, nxt, group_of_tile, tm=512, tf=512, nt=512):
    m, d = x.shape
    ff = w_up.shape[2]
    nf = ff // tf
    mod_spec = pl.BlockSpec((1, SUBLANES, d), lambda i, f: (group_of_tile(i, tm), 0, 0))
    vec_spec = pl.BlockSpec((SUBLANES, d), lambda i, f: (0, 0))
    row_tile = lambda i, f: (i, 0)
    in_specs = [_resident((tm, d), row_tile),
                pl.BlockSpec((1, d, tf), lambda i, f: (li, 0, f)),
                pl.BlockSpec((1, tf, d), lambda i, f: (li, f, 0)),
                _resident((tm, d), row_tile), mod_spec, vec_spec]
    args = [h, w_up, w_down, x, gt, _rep8(g_post)]
    out_specs = [pl.BlockSpec((tm, d), row_tile)]
    out_shape = [jax.ShapeDtypeStruct((m, d), F32)]
    if nxt is not None:
        g_pre, sh, sc = nxt
        in_specs += [vec_spec, mod_spec, mod_spec]
        args += [_rep8(g_pre), sh, sc]
        out_specs.append(pl.BlockSpec((tm, d), row_tile))
        out_shape.append(jax.ShapeDtypeStruct((m, d), BF16))
    outs = pl.pallas_call(
        functools.partial(_mlp_body, nf=nf, nt=nt, tm=tm, has_next=nxt is not None),
        grid=(m // tm, nf),
        in_specs=in_specs,
        out_specs=out_specs,
        out_shape=out_shape,
        scratch_shapes=[pltpu.VMEM((tm, LANES), F32)],
        compiler_params=_params("parallel", "arbitrary"),
        name="mlp" if nxt is not None else "mlp_last",
    )(*args)
    return (outs[0], outs[1]) if nxt is not None else (outs[0], None)


class _Layout:
    def __init__(self, d_model):
        self.attn_heads = d_model // (2 * HEAD_DIM)
        self.kv_heads = self.attn_heads // ATTN_GROUP
        self.ret_heads = d_model // (2 * RET_V_DIM)
        vchunks = RET_V_DIM // LANES
        self.aq0 = 0
        self.ak0 = self.aq0 + self.attn_heads
        self.av0 = self.ak0 + self.kv_heads
        self.rq0 = self.av0 + self.kv_heads
        self.rk0 = self.rq0 + self.ret_heads
        self.rv0 = self.rk0 + self.ret_heads
        self.rg0 = self.rv0 + vchunks * self.ret_heads
        self.n_chunks = self.rg0 + vchunks * self.ret_heads

    def key_value_tiles(self, n_chunk):
        chunks = list(range(self.ak0, self.rq0)) + list(range(self.rk0, self.rg0))
        return tuple(sorted({c // n_chunk for c in chunks}))

    def restricted_to(self, tiles, n_chunk):
        sub = _Layout.__new__(_Layout)
        sub.kv_heads, sub.ret_heads = self.kv_heads, self.ret_heads
        where = lambda c: tiles.index(c // n_chunk) * n_chunk + c % n_chunk
        sub.ak0, sub.av0, sub.rk0, sub.rv0 = (where(c) for c in (self.ak0, self.av0, self.rk0, self.rv0))
        return sub

    def kind_of_chunk(self, c):
        if c < self.ak0:
            return "q"
        if c < self.av0:
            return "rope"
        if c < self.rq0:
            return "none"
        if c < self.rk0:
            return "rope"
        if c < self.rv0:
            return "rk"
        return "none"


def _rope_tables(seq_len):
    pairs = HEAD_DIM // 4
    t = jnp.arange(seq_len)
    r = (t // GRID_W).astype(F32)
    cl = (t % GRID_W).astype(F32)
    inv = ROPE_BASE ** (-jnp.arange(pairs, dtype=F32) / pairs)
    ar = r[:, None] * inv
    ac = cl[:, None] * inv
    ang = jnp.concatenate([ar, ar, ac, ac], axis=-1)
    cos, sin = jnp.cos(ang), jnp.sin(ang)
    first = (jnp.arange(HEAD_DIM) % (2 * pairs)) < pairs
    return cos, jnp.where(first, -sin, 0.0), jnp.where(first, 0.0, sin)


def kernel(x, c, ctx, c_ctx, w_ada, b_ada, norm_g, w_in, attn_sink, ret_log_decay, ret_gn_g, w_out, w_up, w_down):
    batch, seq_len, d = x.shape
    ctx_len = ctx.shape[1]
    depth = w_in.shape[0]
    layout = _Layout(d)
    assert layout.n_chunks * LANES == w_in.shape[2]
    assert batch + 1 <= MOD_ROWS

    xf = x.reshape(batch * seq_len, d)
    cf = ctx.reshape(batch * ctx_len, d)

    c_all = jnp.concatenate([c, c_ctx[None, :], jnp.zeros((MOD_ROWS - batch - 1, d), F32)], axis=0)
    mod = _adaln(c_all, w_ada, b_ada)
    mod = mod.reshape(depth, MOD_ROWS, 6, 1, d).transpose(0, 2, 1, 3, 4)
    mod = jnp.broadcast_to(mod, (depth, 6, MOD_ROWS, SUBLANES, d))
    lat_group = lambda i, tm: (i * tm) // seq_len
    ctx_group = lambda i, tm: batch

    rope_tabs = _rope_tables(seq_len)
    g = ATTN_GROUP
    wi, wo, wu, wd = (w.astype(BF16) for w in (w_in, w_out, w_up, w_down))

    h = _prenorm(xf, norm_g[0, 0], mod[0, 0], mod[0, 1], lat_group)
    hc = _prenorm(cf, norm_g[0, 0], mod[0, 0], mod[0, 1], ctx_group)
    for li in range(depth):
        last = li == depth - 1
        sh1, sc1, gt1, sh2, sc2, gt2 = [mod[li, k] for k in range(6)]
        g_pre_mix, g_post_mix, g_pre_mlp, g_post_mlp = [norm_g[li, k] for k in range(4)]
        sink = attn_sink[li].reshape(layout.kv_heads, g)

        p_lat = _inproj(h, wi, li, layout, rope_tabs, seq_len)
        if last:
            kv_tiles = layout.key_value_tiles(IN_TN // LANES)
            ctx_layout = layout.restricted_to(kv_tiles, IN_TN // LANES)
            p_ctx = _inproj(hc, wi, li, layout, None, ctx_len, tiles=kv_tiles)
        else:
            ctx_layout = layout
            p_ctx = _inproj(hc, wi, li, layout, None, ctx_len)
        sink_lat = jnp.repeat(sink, ATTN_BLOCK, axis=1)[:, None, :]
        att = _attention(p_lat, p_ctx, sink_lat, layout, ctx_layout, batch, seq_len, ctx_len)
        ret, cret = _retention(p_lat, p_ctx, ret_log_decay[li], ret_gn_g[li], layout, ctx_layout,
                               batch, seq_len, ctx_len, ctx_out=not last)
        xf, h2 = _outproj(att, ret, wo, li, xf, gt1, g_post_mix, g_pre_mlp, sh2, sc2, lat_group)
        nxt = None if last else (norm_g[li + 1, 0], mod[li + 1, 0], mod[li + 1, 1])
        xf, h = _mlp(h2, wu, wd, li, xf, gt2, g_post_mlp, nxt, lat_group)
        if not last:
            sink_ctx = jnp.repeat(sink, ctx_len, axis=1)[:, None, :]
            catt = _ctx_attention(p_ctx, sink_ctx, layout, batch, ctx_len)
            cf, hc2 = _outproj(catt, cret, wo, li, cf, gt1, g_post_mix, g_pre_mlp, sh2, sc2, ctx_group)
            cf, hc = _mlp(hc2, wu, wd, li, cf, gt2, g_post_mlp, nxt, ctx_group)
    return xf.reshape(batch, seq_len, d)
```

```python
import functools

import jax
import jax.numpy as jnp
from jax import lax
from jax.experimental import pallas as pl
from jax.experimental.pallas import tpu as pltpu

F32 = jnp.float32
BF16 = jnp.bfloat16

HEAD_DIM = 128
GRID_W = 64
WINDOW = 128
ATTN_BLOCK = 128
ATTN_GROUP = 4
RET_CHUNK = 128
RET_V_DIM = 2 * HEAD_DIM
ROPE_BASE = 10000.0
EPS = 1e-6
MASKED = -1e30

LANES = 128
SUBLANES = 8
BF16_ROWS = 16
MOD_ROWS = 16
VMEM_LIMIT = 56 * 1024 * 1024
IN_TN = 512


def _rep8(v):
    return jnp.broadcast_to(v.reshape(1, -1), (SUBLANES, v.shape[-1]))


def _params(*sem):
    return pltpu.CompilerParams(dimension_semantics=sem, vmem_limit_bytes=VMEM_LIMIT)


def _adaln_body(c_ref, w_ref, b_ref, o_ref):
    c = c_ref[...]
    a = (c * jax.nn.sigmoid(c)).astype(BF16)
    o_ref[0] = jnp.dot(a, w_ref[0].astype(BF16), preferred_element_type=F32) + b_ref[0]


def _adaln(c_all, w_ada, b_ada, tn=512):
    depth, d, n = w_ada.shape
    rows = c_all.shape[0]
    return pl.pallas_call(
        _adaln_body,
        grid=(depth, n // tn),
        in_specs=[pl.BlockSpec((rows, d), lambda l, j: (0, 0)),
                  pl.BlockSpec((1, d, tn), lambda l, j: (l, 0, j)),
                  pl.BlockSpec((1, 1, tn), lambda l, j: (l, 0, j))],
        out_specs=pl.BlockSpec((1, rows, tn), lambda l, j: (l, 0, j)),
        out_shape=jax.ShapeDtypeStruct((depth, rows, n), F32),
        compiler_params=_params("parallel", "parallel"),
        name="adaln",
    )(c_all, w_ada, b_ada.reshape(depth, 1, n))


def _prenorm_body(x_ref, g_ref, sh_ref, sc_ref, o_ref):
    tm, d = x_ref.shape
    x = x_ref[...].reshape(tm // SUBLANES, SUBLANES, d)
    r = lax.rsqrt(jnp.mean(x * x, axis=-1, keepdims=True) + EPS)
    h = ((x * r) * g_ref[...][None]) * (1.0 + sc_ref[...]) + sh_ref[...]
    o_ref[...] = h.reshape(tm, d).astype(BF16)


def _prenorm(x, g, sh, sc, group_of_tile, tm=256):
    m, d = x.shape
    mod_spec = pl.BlockSpec((1, SUBLANES, d), lambda i: (group_of_tile(i, tm), 0, 0))
    return pl.pallas_call(
        _prenorm_body,
        grid=(m // tm,),
        in_specs=[pl.BlockSpec((tm, d), lambda i: (i, 0)),
                  pl.BlockSpec((SUBLANES, d), lambda i: (0, 0)),
                  mod_spec, mod_spec],
        out_specs=pl.BlockSpec((tm, d), lambda i: (i, 0)),
        out_shape=jax.ShapeDtypeStruct((m, d), BF16),
        compiler_params=_params("parallel"),
        name="prenorm",
    )(x, _rep8(g), sh, sc)


INPROJ_KINDS = ("q", "rope", "rk", "none")


def _inproj_body(kind_ref, tile_ref, h_ref, w_ref, c_ref, a_ref, b_ref, o_ref, acc_ref, *, n_steps, n_chunk):
    del kind_ref, tile_ref
    s = pl.program_id(0)
    slot = s % 2

    def multiply():
        acc_ref[slot] = jnp.dot(h_ref[...], w_ref[0], preferred_element_type=F32)

    def finish_previous():
        c, a, b = c_ref[0], a_ref[0], b_ref[0]
        for cc in range(n_chunk):
            t = acc_ref[1 - slot, :, cc * LANES:(cc + 1) * LANES]
            o_ref[cc] = (t * c + (pltpu.roll(t, 96, 1) * a + pltpu.roll(t, 32, 1) * b)).astype(BF16)

    @pl.when(s == 0)
    def _():
        multiply()

    @pl.when(jnp.logical_and(s > 0, s < n_steps))
    def _():
        finish_previous()
        multiply()

    @pl.when(s == n_steps)
    def _():
        finish_previous()


def _inproj_tables(rope_tabs, rows):
    if rope_tabs is None:
        cos = jnp.ones((rows, HEAD_DIM), F32)
        sa = sb = jnp.zeros((rows, HEAD_DIM), F32)
    else:
        cos, sa, sb = rope_tabs
    scale = {"q": HEAD_DIM ** -0.5, "rope": 1.0, "rk": HEAD_DIM ** -0.5}
    one, zero = jnp.ones_like(cos), jnp.zeros_like(cos)
    stack = lambda tab, ident: jnp.stack([ident if k == "none" else tab * scale[k] for k in INPROJ_KINDS])
    return stack(cos, one), stack(sa, zero), stack(sb, zero)


def _inproj(h, w, li, layout, rope_tabs, seq_len, tiles=None, tm=1024):
    m, d = h.shape
    tn = IN_TN
    n_chunk = tn // LANES
    tiles = tuple(range(w.shape[2] // tn)) if tiles is None else tuple(tiles)
    nj = len(tiles)
    kinds = []
    for t in tiles:
        ks = {layout.kind_of_chunk(c) for c in range(t * n_chunk, (t + 1) * n_chunk)}
        assert len(ks) == 1, "column tile must not straddle head groups"
        kinds.append(INPROJ_KINDS.index(ks.pop()))
    tab_rows = seq_len if rope_tabs is not None else tm
    per_seq = tab_rows // tm
    tabs = _inproj_tables(rope_tabs, tab_rows)
    n_steps = (m // tm) * nj
    cur = lambda s: jnp.minimum(s, n_steps - 1)
    prv = lambda s: jnp.maximum(s - 1, 0)
    tab_spec = pl.BlockSpec((1, tm, LANES),
                            lambda s, kind, tile: (kind[prv(s) % nj], (prv(s) // nj) % per_seq, 0))
    return pl.pallas_call(
        functools.partial(_inproj_body, n_steps=n_steps, n_chunk=n_chunk),
        grid_spec=pltpu.PrefetchScalarGridSpec(
            num_scalar_prefetch=2,
            grid=(n_steps + 1,),
            in_specs=[pl.BlockSpec((tm, d), lambda s, kind, tile: (cur(s) // nj, 0)),
                      pl.BlockSpec((1, d, tn), lambda s, kind, tile: (li, 0, tile[cur(s) % nj])),
                      tab_spec, tab_spec, tab_spec],
            out_specs=pl.BlockSpec((n_chunk, tm, LANES), lambda s, kind, tile: (prv(s) % nj, prv(s) // nj, 0)),
            scratch_shapes=[pltpu.VMEM((2, tm, tn), F32)]),
        out_shape=jax.ShapeDtypeStruct((nj * n_chunk, m, LANES), BF16),
        compiler_params=_params("arbitrary"),
        name="inproj_rope" if rope_tabs is not None else "inproj_ctx",
    )(jnp.asarray(kinds, jnp.int32), jnp.asarray(tiles, jnp.int32), h, w, *tabs)


def _attn_body(*refs, local, tq, kvh):
    if local:
        q_ref, kp_ref, kc_ref, kn_ref, vp_ref, vc_ref, vn_ref, ck_ref, cv_ref, sink_ref, bias_ref, o_ref = refs
    else:
        q_ref, ck_ref, cv_ref, sink_ref, o_ref = refs
    g = q_ref.shape[0] // kvh
    rows = g * tq
    nt = (((1,), (1,)), ((), ()))
    tn = (((0,), (0,)), ((), ()))
    for h in range(kvh):
        q = q_ref[h * g:(h + 1) * g].reshape(rows, HEAD_DIM)
        sink = sink_ref[h]
        s_ctx = lax.dot_general(ck_ref[h], q, nt, preferred_element_type=F32)
        m = jnp.maximum(jnp.max(s_ctx, axis=0, keepdims=True), sink)
        if local:
            k_loc = jnp.concatenate([kp_ref[h], kc_ref[h], kn_ref[h]], axis=0)
            s_loc = lax.dot_general(k_loc, q, nt, preferred_element_type=F32) + bias_ref[0]
            m = jnp.maximum(m, jnp.max(s_loc, axis=0, keepdims=True))
        p_ctx = jnp.exp(s_ctx - m)
        denom = jnp.sum(p_ctx, axis=0, keepdims=True) + jnp.exp(sink - m)
        ot = lax.dot_general(cv_ref[h], p_ctx.astype(BF16), tn, preferred_element_type=F32)
        if local:
            p_loc = jnp.exp(s_loc - m)
            denom = denom + jnp.sum(p_loc, axis=0, keepdims=True)
            v_loc = jnp.concatenate([vp_ref[h], vc_ref[h], vn_ref[h]], axis=0)
            ot = ot + lax.dot_general(v_loc, p_loc.astype(BF16), tn, preferred_element_type=F32)
        o = (ot / denom).T
        for gi in range(g):
            col = (h * g + gi) * HEAD_DIM
            o_ref[:, col:col + HEAD_DIM] = o[gi * tq:(gi + 1) * tq].astype(BF16)


def _band_bias(tq, n_qblocks, group):
    assert n_qblocks >= 2
    qq = jnp.tile(jnp.arange(tq), group)[None, :]
    cc = jnp.arange(3 * tq)[:, None]
    off = cc - qq
    band = (off >= 0) & (off <= 2 * WINDOW)
    first = band & (cc >= tq)
    last = band & (cc < 2 * tq)
    return jnp.where(jnp.stack([first, band, last]), 0.0, MASKED).astype(F32)


def _attention(p_lat, p_ctx, sink_row, layout, ctx_layout, batch, seq_len, ctx_len):
    g, tq = ATTN_GROUP, ATTN_BLOCK
    nq = seq_len // tq
    kvh = layout.kv_heads
    assert layout.aq0 == 0 and all(c % kvh == 0 for c in (layout.ak0, layout.av0, ctx_layout.ak0, ctx_layout.av0))
    ak, av = layout.ak0 // kvh, layout.av0 // kvh
    cak, cav = ctx_layout.ak0 // kvh, ctx_layout.av0 // kvh
    prev = lambda b, n: b * nq + jnp.maximum(n - 1, 0)
    cur = lambda b, n: b * nq + n
    nxt = lambda b, n: b * nq + jnp.minimum(n + 1, nq - 1)
    in_specs = [pl.BlockSpec((kvh * g, tq, HEAD_DIM), lambda b, n: (0, cur(b, n), 0))]
    for base in (ak, av):
        for row in (prev, cur, nxt):
            in_specs.append(pl.BlockSpec((kvh, tq, HEAD_DIM), functools.partial(
                lambda b, n, base, row: (base, row(b, n), 0), base=base, row=row)))
    edge = lambda b, n: jnp.where(n == 0, 0, jnp.where(n == nq - 1, 2, 1))
    in_specs += [pl.BlockSpec((kvh, ctx_len, HEAD_DIM), lambda b, n: (cak, b, 0)),
                 pl.BlockSpec((kvh, ctx_len, HEAD_DIM), lambda b, n: (cav, b, 0)),
                 pl.BlockSpec((kvh, 1, g * tq), lambda b, n: (0, 0, 0)),
                 pl.BlockSpec((1, 3 * tq, g * tq), lambda b, n: (edge(b, n), 0, 0))]
    return pl.pallas_call(
        functools.partial(_attn_body, local=True, tq=tq, kvh=kvh),
        grid=(batch, nq),
        in_specs=in_specs,
        out_specs=pl.BlockSpec((tq, kvh * g * HEAD_DIM), lambda b, n: (cur(b, n), 0)),
        out_shape=jax.ShapeDtypeStruct((batch * seq_len, kvh * g * HEAD_DIM), BF16),
        compiler_params=_params("parallel", "parallel"),
        name="window_attention",
    )(p_lat, *([p_lat] * 6), p_ctx, p_ctx, sink_row, _band_bias(tq, nq, g))


def _ctx_attention(p_ctx, sink_row, layout, batch, ctx_len):
    g = ATTN_GROUP
    kvh = layout.kv_heads
    ak, av = layout.ak0 // kvh, layout.av0 // kvh
    return pl.pallas_call(
        functools.partial(_attn_body, local=False, tq=ctx_len, kvh=kvh),
        grid=(batch,),
        in_specs=[pl.BlockSpec((kvh * g, ctx_len, HEAD_DIM), lambda b: (0, b, 0)),
                  pl.BlockSpec((kvh, ctx_len, HEAD_DIM), lambda b: (ak, b, 0)),
                  pl.BlockSpec((kvh, ctx_len, HEAD_DIM), lambda b: (av, b, 0)),
                  pl.BlockSpec((kvh, 1, g * ctx_len), lambda b: (0, 0, 0))],
        out_specs=pl.BlockSpec((ctx_len, kvh * g * HEAD_DIM), lambda b: (b, 0)),
        out_shape=jax.ShapeDtypeStruct((batch * ctx_len, kvh * g * HEAD_DIM), BF16),
        compiler_params=_params("parallel"),
        name="context_attention",
    )(p_ctx, p_ctx, p_ctx, sink_row)


def _ret_body(lg_ref, q_ref, k_ref, v_ref, g_ref, *rest, n_lat, n_ctx, ctx_out):
    if ctx_out:
        cq_ref, ck_ref, cv_ref, cg_ref, gn_ref, o_ref, co_ref, yf_ref = rest
    else:
        ck_ref, cv_ref, gn_ref, o_ref, yf_ref = rest
        cq_ref = cg_ref = co_ref = None
    c = RET_CHUNK
    h = pl.program_id(1)
    lgf = lg_ref[0, h]
    lgb = lg_ref[1, h]
    ii = lax.broadcasted_iota(jnp.int32, (c, c), 0)
    jj = lax.broadcasted_iota(jnp.int32, (c, c), 1)
    rel = (ii - jj).astype(F32)
    dm_f = jnp.where(rel >= 0, jnp.exp(lgf * jnp.maximum(rel, 0.0)), 0.0)
    dm_b = jnp.where(rel < 0, jnp.exp(lgb * jnp.maximum(-rel, 0.0)), 0.0)
    pos = lax.broadcasted_iota(jnp.int32, (c, HEAD_DIM), 0).astype(F32)
    qd_f = jnp.exp(lgf * (pos + 1.0))
    kd_f = jnp.exp(lgf * (c - 1.0 - pos))
    qd_b = jnp.exp(lgb * (c - pos))
    kd_b = jnp.exp(lgb * pos)
    cd_f = jnp.exp(jnp.full((1, RET_V_DIM), lgf * c, F32))
    cd_b = jnp.exp(jnp.full((1, RET_V_DIM), lgb * c, F32))
    gn = gn_ref[0]

    def rows(ci):
        return slice(ci * c, (ci + 1) * c)

    def wide(ref, ci):
        return jnp.concatenate([ref[0, rows(ci), :], ref[1, rows(ci), :]], axis=1)

    def step(q, k, v, state, dm, qd, kd, cd):
        kv = lax.dot_general((k.astype(F32) * kd).astype(BF16), v, (((0,), (0,)), ((), ())),
                             preferred_element_type=F32)
        out = None
        if q is not None:
            s = lax.dot_general(q, k, (((1,), (1,)), ((), ())), preferred_element_type=F32) * dm
            out = jnp.dot(s.astype(BF16), v, preferred_element_type=F32)
            if state is not None:
                out = out + jnp.dot((q.astype(F32) * qd).astype(BF16), state.astype(BF16),
                                    preferred_element_type=F32)
        return out, (kv if state is None else state * cd + kv)

    def finish(y, gate):
        mu = jnp.mean(y, axis=-1, keepdims=True)
        dlt = y - mu
        var = jnp.mean(dlt * dlt, axis=-1, keepdims=True)
        yn = dlt * lax.rsqrt(var + EPS) * gn
        gf = gate.astype(F32)
        return ((gf * jax.nn.sigmoid(gf)) * yn).astype(BF16)

    n_y_ctx = n_ctx if ctx_out else 0

    state = None
    for ci in range(n_ctx):
        q = cq_ref[0, rows(ci), :] if ctx_out else None
        out, state = step(q, ck_ref[0, rows(ci), :], wide(cv_ref, ci), state, dm_f, qd_f, kd_f, cd_f)
        if ctx_out:
            yf_ref[rows(ci), :] = out
    for ci in range(n_lat):
        out, state = step(q_ref[0, rows(ci), :], k_ref[0, rows(ci), :], wide(v_ref, ci), state,
                          dm_f, qd_f, kd_f, cd_f)
        yf_ref[rows(n_y_ctx + ci), :] = out

    state = None
    for ci in reversed(range(n_ctx)):
        q = cq_ref[0, rows(ci), :] if ctx_out else None
        out, state = step(q, ck_ref[0, rows(ci), :], wide(cv_ref, ci), state, dm_b, qd_b, kd_b, cd_b)
        if ctx_out:
            co_ref[rows(ci), :] = finish(yf_ref[rows(ci), :] + out, wide(cg_ref, ci))
    for ci in reversed(range(n_lat)):
        out, state = step(q_ref[0, rows(ci), :], k_ref[0, rows(ci), :], wide(v_ref, ci), state,
                          dm_b, qd_b, kd_b, cd_b)
        o_ref[rows(ci), :] = finish(yf_ref[rows(n_y_ctx + ci), :] + out, wide(g_ref, ci))


def _retention(p_lat, p_ctx, log_decay, gn_g, layout, ctx_layout, batch, seq_len, ctx_len, ctx_out):
    nh = layout.ret_heads
    vchunks = RET_V_DIM // LANES
    narrow = lambda base, ln: pl.BlockSpec((1, ln, HEAD_DIM), lambda b, h, lg: (base + h, b, 0))
    wide = lambda base, ln: pl.BlockSpec((vchunks, ln, HEAD_DIM), lambda b, h, lg: (base // vchunks + h, b, 0))
    in_specs = [narrow(layout.rq0, seq_len), narrow(layout.rk0, seq_len),
                wide(layout.rv0, seq_len), wide(layout.rg0, seq_len)]
    args = [p_lat] * 4
    if ctx_out:
        in_specs += [narrow(ctx_layout.rq0, ctx_len), narrow(ctx_layout.rk0, ctx_len),
                     wide(ctx_layout.rv0, ctx_len), wide(ctx_layout.rg0, ctx_len)]
        args += [p_ctx] * 4
    else:
        in_specs += [narrow(ctx_layout.rk0, ctx_len), wide(ctx_layout.rv0, ctx_len)]
        args += [p_ctx] * 2
    in_specs.append(pl.BlockSpec((1, 1, RET_V_DIM), lambda b, h, lg: (h, 0, 0)))
    args.append(gn_g.reshape(nh, 1, RET_V_DIM))
    out_specs = [pl.BlockSpec((seq_len, RET_V_DIM), lambda b, h, lg: (b, h))]
    out_shape = [jax.ShapeDtypeStruct((batch * seq_len, nh * RET_V_DIM), BF16)]
    y_rows = seq_len
    if ctx_out:
        out_specs.append(pl.BlockSpec((ctx_len, RET_V_DIM), lambda b, h, lg: (b, h)))
        out_shape.append(jax.ShapeDtypeStruct((batch * ctx_len, nh * RET_V_DIM), BF16))
        y_rows += ctx_len
    outs = pl.pallas_call(
        functools.partial(_ret_body, n_lat=seq_len // RET_CHUNK, n_ctx=ctx_len // RET_CHUNK, ctx_out=ctx_out),
        grid_spec=pltpu.PrefetchScalarGridSpec(
            num_scalar_prefetch=1,
            grid=(batch, nh),
            in_specs=in_specs,
            out_specs=out_specs,
            scratch_shapes=[pltpu.VMEM((y_rows, RET_V_DIM), F32)]),
        out_shape=out_shape,
        compiler_params=_params("parallel", "parallel"),
        name="retention" if ctx_out else "retention_last",
    )(log_decay, *args)
    return (outs[0], outs[1]) if ctx_out else (outs[0], None)


def _residual_epilogue(xo_ref, x_ref, gt_ref, gpost_ref, nxt, part_ref, tm):
    d = xo_ref.shape[1]
    nc = d // LANES
    blk = (BF16_ROWS // SUBLANES, SUBLANES, LANES)
    n_blocks = tm // BF16_ROWS

    def rows_of(rb):
        return pl.ds(pl.multiple_of(rb * BF16_ROWS, BF16_ROWS), BF16_ROWS)

    def cols(c):
        return slice(c * LANES, (c + 1) * LANES)

    def partials_to_rsqrt():
        ss = jnp.sum(part_ref[...], axis=-1, keepdims=True)
        part_ref[...] = jnp.broadcast_to(lax.rsqrt(ss / d + EPS), part_ref.shape)

    def sweep_sumsq(rb, carry):
        rws = rows_of(rb)
        acc = [jnp.zeros((BF16_ROWS, LANES), F32), jnp.zeros((BF16_ROWS, LANES), F32)]
        for c in range(nc):
            y = xo_ref[rws, cols(c)]
            acc[c % 2] = acc[c % 2] + y * y
        part_ref[rws, :] = acc[0] + acc[1]
        return carry

    def sweep_residual(rb, carry):
        rws = rows_of(rb)
        r = part_ref[rws, :].reshape(blk)
        acc = [jnp.zeros(blk, F32), jnp.zeros(blk, F32)]
        for c in range(nc):
            y = xo_ref[rws, cols(c)].reshape(blk)
            xn = x_ref[rws, cols(c)].reshape(blk) + gt_ref[:, :, cols(c)] * ((y * r) * gpost_ref[:, cols(c)][None])
            xo_ref[rws, cols(c)] = xn.reshape(BF16_ROWS, LANES)
            acc[c % 2] = acc[c % 2] + xn * xn
        part_ref[rws, :] = (acc[0] + acc[1]).reshape(BF16_ROWS, LANES)
        return carry

    lax.fori_loop(0, n_blocks, sweep_sumsq, 0, unroll=2)
    partials_to_rsqrt()
    lax.fori_loop(0, n_blocks, sweep_residual, 0, unroll=2)
    if nxt is None:
        return
    gpre_ref, sh_ref, sc_ref, hn_ref = nxt
    sc1 = 1.0 + sc_ref[...]

    def sweep_prenorm(rb, carry):
        rws = rows_of(rb)
        r2 = part_ref[rws, :].reshape(blk)
        for c in range(nc):
            xn = xo_ref[rws, cols(c)].reshape(blk)
            hn = ((xn * r2) * gpre_ref[:, cols(c)][None]) * sc1[:, :, cols(c)] + sh_ref[:, :, cols(c)]
            hn_ref[rws, cols(c)] = hn.reshape(BF16_ROWS, LANES).astype(BF16)
        return carry

    partials_to_rsqrt()
    lax.fori_loop(0, n_blocks, sweep_prenorm, 0, unroll=2)


def _outproj_body(a_ref, r_ref, wa_ref, wb_ref, x_ref, gt_ref, gpost_ref, gpre_ref, sh_ref, sc_ref,
                  xo_ref, hn_ref, part_ref, *, nj, tn, tm):
    j = pl.program_id(1)
    y = (jnp.dot(a_ref[...], wa_ref[0], preferred_element_type=F32)
         + jnp.dot(r_ref[...], wb_ref[0], preferred_element_type=F32))
    for jj in range(nj):
        @pl.when(j == jj)
        def _():
            xo_ref[:, jj * tn:(jj + 1) * tn] = y

    @pl.when(j == nj - 1)
    def _():
        _residual_epilogue(xo_ref, x_ref, gt_ref, gpost_ref, (gpre_ref, sh_ref, sc_ref, hn_ref), part_ref, tm)


def _resident(shape, index_map):
    return pl.BlockSpec(shape, index_map, pipeline_mode=pl.Buffered(1))


def _outproj(att, ret, w_out, li, x, gt, g_post, g_pre, sh, sc, group_of_tile, tm=512, tn=512):
    m, d = x.shape
    ka = att.shape[1]
    assert ret.shape[1] == ka and w_out.shape[1] == 2 * ka
    nj = d // tn
    mod_spec = pl.BlockSpec((1, SUBLANES, d), lambda i, j: (group_of_tile(i, tm), 0, 0))
    vec_spec = pl.BlockSpec((SUBLANES, d), lambda i, j: (0, 0))
    row_tile = lambda i, j: (i, 0)
    return pl.pallas_call(
        functools.partial(_outproj_body, nj=nj, tn=tn, tm=tm),
        grid=(m // tm, nj),
        in_specs=[pl.BlockSpec((tm, ka), row_tile), pl.BlockSpec((tm, ka), row_tile),
                  pl.BlockSpec((1, ka, tn), lambda i, j: (li, 0, j)),
                  pl.BlockSpec((1, ka, tn), lambda i, j: (li, 1, j)),
                  _resident((tm, d), row_tile), mod_spec, vec_spec, vec_spec, mod_spec, mod_spec],
        out_specs=[pl.BlockSpec((tm, d), row_tile), pl.BlockSpec((tm, d), row_tile)],
        out_shape=[jax.ShapeDtypeStruct((m, d), F32), jax.ShapeDtypeStruct((m, d), BF16)],
        scratch_shapes=[pltpu.VMEM((tm, LANES), F32)],
        compiler_params=_params("parallel", "arbitrary"),
        name="outproj",
    )(att, ret, w_out, w_out, x, gt, _rep8(g_post), _rep8(g_pre), sh, sc)


def _mlp_body(h_ref, wu_ref, wd_ref, x_ref, gt_ref, gpost_ref, *rest, nf, nt, tm, has_next):
    if has_next:
        gpre_ref, sh_ref, sc_ref, xo_ref, hn_ref, part_ref, act_ref = rest
        nxt = (gpre_ref, sh_ref, sc_ref, hn_ref)
    else:
        xo_ref, part_ref, act_ref = rest
        nxt = None
    f = pl.program_id(1)
    slot = f % 2

    def up():
        u = jnp.maximum(jnp.dot(h_ref[...], wu_ref[0], preferred_element_type=F32), 0.0)
        act_ref[slot] = (u * u).astype(BF16)

    def down():
        a = act_ref[1 - slot]
        for nn in range(xo_ref.shape[1] // nt):
            cols = slice(nn * nt, (nn + 1) * nt)
            xo_ref[:, cols] += jnp.dot(a, wd_ref[0, :, cols], preferred_element_type=F32)

    @pl.when(f == 0)
    def _():
        xo_ref[...] = jnp.zeros_like(xo_ref)
        up()

    @pl.when(jnp.logical_and(f > 0, f < nf))
    def _():
        down()
        up()

    @pl.when(f == nf)
    def _():
        down()
        _residual_epilogue(xo_ref, x_ref, gt_ref, gpost_ref, nxt, part_ref, tm)


def _mlp(h, w_up, w_down, li, x, gt, g_post, nxt, group_of_tile, tm=512, tf=512, nt=512):
    m, d = x.shape
    ff = w_up.shape[2]
    nf = ff // tf
    mod_spec = pl.BlockSpec((1, SUBLANES, d), lambda i, f: (group_of_tile(i, tm), 0, 0))
    vec_spec = pl.BlockSpec((SUBLANES, d), lambda i, f: (0, 0))
    row_tile = lambda i, f: (i, 0)
    in_specs = [_resident((tm, d), row_tile),
                pl.BlockSpec((1, d, tf), lambda i, f: (li, 0, jnp.minimum(f, nf - 1))),
                pl.BlockSpec((1, tf, d), lambda i, f: (li, jnp.maximum(f - 1, 0), 0)),
                _resident((tm, d), row_tile), mod_spec, vec_spec]
    args = [h, w_up, w_down, x, gt, _rep8(g_post)]
    out_specs = [pl.BlockSpec((tm, d), row_tile)]
    out_shape = [jax.ShapeDtypeStruct((m, d), F32)]
    if nxt is not None:
        g_pre, sh, sc = nxt
        in_specs += [vec_spec, mod_spec, mod_spec]
        args += [_rep8(g_pre), sh, sc]
        out_specs.append(pl.BlockSpec((tm, d), row_tile))
        out_shape.append(jax.ShapeDtypeStruct((m, d), BF16))
    outs = pl.pallas_call(
        functools.partial(_mlp_body, nf=nf, nt=nt, tm=tm, has_next=nxt is not None),
        grid=(m // tm, nf + 1),
        in_specs=in_specs,
        out_specs=out_specs,
        out_shape=out_shape,
        scratch_shapes=[pltpu.VMEM((tm, LANES), F32), pltpu.VMEM((2, tm, tf), BF16)],
        compiler_params=_params("parallel", "arbitrary"),
        name="mlp" if nxt is not None else "mlp_last",
    )(*args)
    return (outs[0], outs[1]) if nxt is not None else (outs[0], None)


class _Layout:
    def __init__(self, d_model):
        self.attn_heads = d_model // (2 * HEAD_DIM)
        self.kv_heads = self.attn_heads // ATTN_GROUP
        self.ret_heads = d_model // (2 * RET_V_DIM)
        vchunks = RET_V_DIM // LANES
        self.aq0 = 0
        self.ak0 = self.aq0 + self.attn_heads
        self.av0 = self.ak0 + self.kv_heads
        self.rq0 = self.av0 + self.kv_heads
        self.rk0 = self.rq0 + self.ret_heads
        self.rv0 = self.rk0 + self.ret_heads
        self.rg0 = self.rv0 + vchunks * self.ret_heads
        self.n_chunks = self.rg0 + vchunks * self.ret_heads

    def key_value_tiles(self, n_chunk):
        chunks = list(range(self.ak0, self.rq0)) + list(range(self.rk0, self.rg0))
        return tuple(sorted({c // n_chunk for c in chunks}))

    def restricted_to(self, tiles, n_chunk):
        sub = _Layout.__new__(_Layout)
        sub.kv_heads, sub.ret_heads = self.kv_heads, self.ret_heads
        where = lambda c: tiles.index(c // n_chunk) * n_chunk + c % n_chunk
        sub.ak0, sub.av0, sub.rk0, sub.rv0 = (where(c) for c in (self.ak0, self.av0, self.rk0, self.rv0))
        return sub

    def kind_of_chunk(self, c):
        if c < self.ak0:
            return "q"
        if c < self.av0:
            return "rope"
        if c < self.rq0:
            return "none"
        if c < self.rk0:
            return "rope"
        if c < self.rv0:
            return "rk"
        return "none"


def _rope_tables(seq_len):
    pairs = HEAD_DIM // 4
    t = jnp.arange(seq_len)
    r = (t // GRID_W).astype(F32)
    cl = (t % GRID_W).astype(F32)
    inv = ROPE_BASE ** (-jnp.arange(pairs, dtype=F32) / pairs)
    ar = r[:, None] * inv
    ac = cl[:, None] * inv
    ang = jnp.concatenate([ar, ar, ac, ac], axis=-1)
    cos, sin = jnp.cos(ang), jnp.sin(ang)
    first = (jnp.arange(HEAD_DIM) % (2 * pairs)) < pairs
    return cos, jnp.where(first, -sin, 0.0), jnp.where(first, 0.0, sin)


def kernel(x, c, ctx, c_ctx, w_ada, b_ada, norm_g, w_in, attn_sink, ret_log_decay, ret_gn_g, w_out, w_up, w_down):
    batch, seq_len, d = x.shape
    ctx_len = ctx.shape[1]
    depth = w_in.shape[0]
    layout = _Layout(d)
    assert layout.n_chunks * LANES == w_in.shape[2]
    assert batch + 1 <= MOD_ROWS

    xf = x.reshape(batch * seq_len, d)
    cf = ctx.reshape(batch * ctx_len, d)

    c_all = jnp.concatenate([c, c_ctx[None, :], jnp.zeros((MOD_ROWS - batch - 1, d), F32)], axis=0)
    mod = _adaln(c_all, w_ada, b_ada)
    mod = mod.reshape(depth, MOD_ROWS, 6, 1, d).transpose(0, 2, 1, 3, 4)
    mod = jnp.broadcast_to(mod, (depth, 6, MOD_ROWS, SUBLANES, d))
    lat_group = lambda i, tm: (i * tm) // seq_len
    ctx_group = lambda i, tm: batch

    rope_tabs = _rope_tables(seq_len)
    g = ATTN_GROUP
    wi, wo, wu, wd = (w.astype(BF16) for w in (w_in, w_out, w_up, w_down))

    h = _prenorm(xf, norm_g[0, 0], mod[0, 0], mod[0, 1], lat_group)
    hc = _prenorm(cf, norm_g[0, 0], mod[0, 0], mod[0, 1], ctx_group)
    for li in range(depth):
        last = li == depth - 1
        sh1, sc1, gt1, sh2, sc2, gt2 = [mod[li, k] for k in range(6)]
        g_pre_mix, g_post_mix, g_pre_mlp, g_post_mlp = [norm_g[li, k] for k in range(4)]
        sink = attn_sink[li].reshape(layout.kv_heads, g)

        p_lat = _inproj(h, wi, li, layout, rope_tabs, seq_len)
        if last:
            kv_tiles = layout.key_value_tiles(IN_TN // LANES)
            ctx_layout = layout.restricted_to(kv_tiles, IN_TN // LANES)
            p_ctx = _inproj(hc, wi, li, layout, None, ctx_len, tiles=kv_tiles)
        else:
            ctx_layout = layout
            p_ctx = _inproj(hc, wi, li, layout, None, ctx_len)
        sink_lat = jnp.repeat(sink, ATTN_BLOCK, axis=1)[:, None, :]
        att = _attention(p_lat, p_ctx, sink_lat, layout, ctx_layout, batch, seq_len, ctx_len)
        ret, cret = _retention(p_lat, p_ctx, ret_log_decay[li], ret_gn_g[li], layout, ctx_layout,
                               batch, seq_len, ctx_len, ctx_out=not last)
        xf, h2 = _outproj(att, ret, wo, li, xf, gt1, g_post_mix, g_pre_mlp, sh2, sc2, lat_group)
        nxt = None if last else (norm_g[li + 1, 0], mod[li + 1, 0], mod[li + 1, 1])
        xf, h = _mlp(h2, wu, wd, li, xf, gt2, g_post_mlp, nxt, lat_group)
        if not last:
            sink_ctx = jnp.repeat(sink, ctx_len, axis=1)[:, None, :]
            catt = _ctx_attention(p_ctx, sink_ctx, layout, batch, ctx_len)
            cf, hc2 = _outproj(catt, cret, wo, li, cf, gt1, g_post_mix, g_pre_mlp, sh2, sc2, ctx_group)
            cf, hc = _mlp(hc2, wu, wd, li, cf, gt2, g_post_mlp, nxt, ctx_group)
    return xf.reshape(batch, seq_len, d)
```

```python
import functools

import jax
import jax.numpy as jnp
from jax import lax
from jax.experimental import pallas as pl
from jax.experimental.pallas import tpu as pltpu

F32 = jnp.float32
BF16 = jnp.bfloat16

HEAD_DIM = 128
GRID_W = 64
WINDOW = 128
ATTN_BLOCK = 128
ATTN_GROUP = 4
RET_CHUNK = 128
RET_V_DIM = 2 * HEAD_DIM
ROPE_BASE = 10000.0
EPS = 1e-6
MASKED = -1e30

LANES = 128
SUBLANES = 8
BF16_ROWS = 16
MOD_ROWS = 16
VMEM_LIMIT = 56 * 1024 * 1024
W_SLOTS = 3
IN_TN = 512


def _rep8(v):
    return jnp.broadcast_to(v.reshape(1, -1), (SUBLANES, v.shape[-1]))


def _params(*sem):
    return pltpu.CompilerParams(dimension_semantics=sem, vmem_limit_bytes=VMEM_LIMIT)


def _adaln_body(c_ref, w_ref, b_ref, o_ref):
    c = c_ref[...]
    a = (c * jax.nn.sigmoid(c)).astype(BF16)
    o_ref[0] = jnp.dot(a, w_ref[0].astype(BF16), preferred_element_type=F32) + b_ref[0]


def _adaln(c_all, w_ada, b_ada, tn=512):
    depth, d, n = w_ada.shape
    rows = c_all.shape[0]
    return pl.pallas_call(
        _adaln_body,
        grid=(depth, n // tn),
        in_specs=[pl.BlockSpec((rows, d), lambda l, j: (0, 0)),
                  pl.BlockSpec((1, d, tn), lambda l, j: (l, 0, j)),
                  pl.BlockSpec((1, 1, tn), lambda l, j: (l, 0, j))],
        out_specs=pl.BlockSpec((1, rows, tn), lambda l, j: (l, 0, j)),
        out_shape=jax.ShapeDtypeStruct((depth, rows, n), F32),
        compiler_params=_params("parallel", "parallel"),
        name="adaln",
    )(c_all, w_ada, b_ada.reshape(depth, 1, n))


def _prenorm_body(x_ref, g_ref, sh_ref, sc_ref, o_ref):
    tm, d = x_ref.shape
    x = x_ref[...].reshape(tm // SUBLANES, SUBLANES, d)
    r = lax.rsqrt(jnp.mean(x * x, axis=-1, keepdims=True) + EPS)
    h = ((x * r) * g_ref[...][None]) * (1.0 + sc_ref[...]) + sh_ref[...]
    o_ref[...] = h.reshape(tm, d).astype(BF16)


def _prenorm(x, g, sh, sc, group_of_tile, tm=256):
    m, d = x.shape
    mod_spec = pl.BlockSpec((1, SUBLANES, d), lambda i: (group_of_tile(i, tm), 0, 0))
    return pl.pallas_call(
        _prenorm_body,
        grid=(m // tm,),
        in_specs=[pl.BlockSpec((tm, d), lambda i: (i, 0)),
                  pl.BlockSpec((SUBLANES, d), lambda i: (0, 0)),
                  mod_spec, mod_spec],
        out_specs=pl.BlockSpec((tm, d), lambda i: (i, 0)),
        out_shape=jax.ShapeDtypeStruct((m, d), BF16),
        compiler_params=_params("parallel"),
        name="prenorm",
    )(x, _rep8(g), sh, sc)


INPROJ_KINDS = ("q", "rope", "rk", "none")


def _inproj_body(kind_ref, tile_ref, h_ref, w_ref, c_ref, a_ref, b_ref, o_ref, acc_ref, *, n_steps, n_chunk):
    del kind_ref, tile_ref
    s = pl.program_id(0)
    slot = s % 2

    def multiply():
        acc_ref[slot] = jnp.dot(h_ref[...], w_ref[0], preferred_element_type=F32)

    def finish_previous():
        c, a, b = c_ref[0], a_ref[0], b_ref[0]
        for cc in range(n_chunk):
            t = acc_ref[1 - slot, :, cc * LANES:(cc + 1) * LANES]
            o_ref[cc] = (t * c + (pltpu.roll(t, 96, 1) * a + pltpu.roll(t, 32, 1) * b)).astype(BF16)

    @pl.when(s == 0)
    def _():
        multiply()

    @pl.when(jnp.logical_and(s > 0, s < n_steps))
    def _():
        finish_previous()
        multiply()

    @pl.when(s == n_steps)
    def _():
        finish_previous()


def _inproj_tables(rope_tabs, rows):
    if rope_tabs is None:
        cos = jnp.ones((rows, HEAD_DIM), F32)
        sa = sb = jnp.zeros((rows, HEAD_DIM), F32)
    else:
        cos, sa, sb = rope_tabs
    scale = {"q": HEAD_DIM ** -0.5, "rope": 1.0, "rk": HEAD_DIM ** -0.5}
    one, zero = jnp.ones_like(cos), jnp.zeros_like(cos)
    stack = lambda tab, ident: jnp.stack([ident if k == "none" else tab * scale[k] for k in INPROJ_KINDS])
    return stack(cos, one), stack(sa, zero), stack(sb, zero)


def _inproj(h, w, li, layout, rope_tabs, seq_len, tiles=None, tm=1024):
    m, d = h.shape
    tn = IN_TN
    n_chunk = tn // LANES
    tiles = tuple(range(w.shape[2] // tn)) if tiles is None else tuple(tiles)
    nj = len(tiles)
    kinds = []
    for t in tiles:
        ks = {layout.kind_of_chunk(c) for c in range(t * n_chunk, (t + 1) * n_chunk)}
        assert len(ks) == 1, "column tile must not straddle head groups"
        kinds.append(INPROJ_KINDS.index(ks.pop()))
    tab_rows = seq_len if rope_tabs is not None else tm
    per_seq = tab_rows // tm
    tabs = _inproj_tables(rope_tabs, tab_rows)
    n_steps = (m // tm) * nj
    cur = lambda s: jnp.minimum(s, n_steps - 1)
    prv = lambda s: jnp.maximum(s - 1, 0)
    tab_spec = pl.BlockSpec((1, tm, LANES),
                            lambda s, kind, tile: (kind[prv(s) % nj], (prv(s) // nj) % per_seq, 0))
    return pl.pallas_call(
        functools.partial(_inproj_body, n_steps=n_steps, n_chunk=n_chunk),
        grid_spec=pltpu.PrefetchScalarGridSpec(
            num_scalar_prefetch=2,
            grid=(n_steps + 1,),
            in_specs=[pl.BlockSpec((tm, d), lambda s, kind, tile: (cur(s) // nj, 0)),
                      pl.BlockSpec((1, d, tn), lambda s, kind, tile: (li, 0, tile[cur(s) % nj])),
                      tab_spec, tab_spec, tab_spec],
            out_specs=pl.BlockSpec((n_chunk, tm, LANES), lambda s, kind, tile: (prv(s) % nj, prv(s) // nj, 0)),
            scratch_shapes=[pltpu.VMEM((2, tm, tn), F32)]),
        out_shape=jax.ShapeDtypeStruct((nj * n_chunk, m, LANES), BF16),
        compiler_params=_params("arbitrary"),
        name="inproj_rope" if rope_tabs is not None else "inproj_ctx",
    )(jnp.asarray(kinds, jnp.int32), jnp.asarray(tiles, jnp.int32), h, w, *tabs)


def _attn_body(*refs, local, tq, kvh):
    if local:
        q_ref, kp_ref, kc_ref, kn_ref, vp_ref, vc_ref, vn_ref, ck_ref, cv_ref, sink_ref, bias_ref, o_ref = refs
    else:
        q_ref, ck_ref, cv_ref, sink_ref, o_ref = refs
    g = q_ref.shape[0] // kvh
    rows = g * tq
    nt = (((1,), (1,)), ((), ()))
    tn = (((0,), (0,)), ((), ()))
    for h in range(kvh):
        q = q_ref[h * g:(h + 1) * g].reshape(rows, HEAD_DIM)
        sink = sink_ref[h]
        s_ctx = lax.dot_general(ck_ref[h], q, nt, preferred_element_type=F32)
        m = jnp.maximum(jnp.max(s_ctx, axis=0, keepdims=True), sink)
        if local:
            k_loc = jnp.concatenate([kp_ref[h], kc_ref[h], kn_ref[h]], axis=0)
            s_loc = lax.dot_general(k_loc, q, nt, preferred_element_type=F32) + bias_ref[0]
            m = jnp.maximum(m, jnp.max(s_loc, axis=0, keepdims=True))
        p_ctx = jnp.exp(s_ctx - m)
        denom = jnp.sum(p_ctx, axis=0, keepdims=True) + jnp.exp(sink - m)
        ot = lax.dot_general(cv_ref[h], p_ctx.astype(BF16), tn, preferred_element_type=F32)
        if local:
            p_loc = jnp.exp(s_loc - m)
            denom = denom + jnp.sum(p_loc, axis=0, keepdims=True)
            v_loc = jnp.concatenate([vp_ref[h], vc_ref[h], vn_ref[h]], axis=0)
            ot = ot + lax.dot_general(v_loc, p_loc.astype(BF16), tn, preferred_element_type=F32)
        o = (ot / denom).T
        for gi in range(g):
            col = (h * g + gi) * HEAD_DIM
            o_ref[:, col:col + HEAD_DIM] = o[gi * tq:(gi + 1) * tq].astype(BF16)


def _band_bias(tq, n_qblocks, group):
    assert n_qblocks >= 2
    qq = jnp.tile(jnp.arange(tq), group)[None, :]
    cc = jnp.arange(3 * tq)[:, None]
    off = cc - qq
    band = (off >= 0) & (off <= 2 * WINDOW)
    first = band & (cc >= tq)
    last = band & (cc < 2 * tq)
    return jnp.where(jnp.stack([first, band, last]), 0.0, MASKED).astype(F32)


def _attention(p_lat, p_ctx, sink_row, layout, ctx_layout, batch, seq_len, ctx_len):
    g, tq = ATTN_GROUP, ATTN_BLOCK
    nq = seq_len // tq
    kvh = layout.kv_heads
    assert layout.aq0 == 0 and all(c % kvh == 0 for c in (layout.ak0, layout.av0, ctx_layout.ak0, ctx_layout.av0))
    ak, av = layout.ak0 // kvh, layout.av0 // kvh
    cak, cav = ctx_layout.ak0 // kvh, ctx_layout.av0 // kvh
    prev = lambda b, n: b * nq + jnp.maximum(n - 1, 0)
    cur = lambda b, n: b * nq + n
    nxt = lambda b, n: b * nq + jnp.minimum(n + 1, nq - 1)
    in_specs = [pl.BlockSpec((kvh * g, tq, HEAD_DIM), lambda b, n: (0, cur(b, n), 0))]
    for base in (ak, av):
        for row in (prev, cur, nxt):
            in_specs.append(pl.BlockSpec((kvh, tq, HEAD_DIM), functools.partial(
                lambda b, n, base, row: (base, row(b, n), 0), base=base, row=row)))
    edge = lambda b, n: jnp.where(n == 0, 0, jnp.where(n == nq - 1, 2, 1))
    in_specs += [pl.BlockSpec((kvh, ctx_len, HEAD_DIM), lambda b, n: (cak, b, 0)),
                 pl.BlockSpec((kvh, ctx_len, HEAD_DIM), lambda b, n: (cav, b, 0)),
                 pl.BlockSpec((kvh, 1, g * tq), lambda b, n: (0, 0, 0)),
                 pl.BlockSpec((1, 3 * tq, g * tq), lambda b, n: (edge(b, n), 0, 0))]
    return pl.pallas_call(
        functools.partial(_attn_body, local=True, tq=tq, kvh=kvh),
        grid=(batch, nq),
        in_specs=in_specs,
        out_specs=pl.BlockSpec((tq, kvh * g * HEAD_DIM), lambda b, n: (cur(b, n), 0)),
        out_shape=jax.ShapeDtypeStruct((batch * seq_len, kvh * g * HEAD_DIM), BF16),
        compiler_params=_params("parallel", "parallel"),
        name="window_attention",
    )(p_lat, *([p_lat] * 6), p_ctx, p_ctx, sink_row, _band_bias(tq, nq, g))


def _ctx_attention(p_ctx, sink_row, layout, batch, ctx_len):
    g = ATTN_GROUP
    kvh = layout.kv_heads
    ak, av = layout.ak0 // kvh, layout.av0 // kvh
    return pl.pallas_call(
        functools.partial(_attn_body, local=False, tq=ctx_len, kvh=kvh),
        grid=(batch,),
        in_specs=[pl.BlockSpec((kvh * g, ctx_len, HEAD_DIM), lambda b: (0, b, 0)),
                  pl.BlockSpec((kvh, ctx_len, HEAD_DIM), lambda b: (ak, b, 0)),
                  pl.BlockSpec((kvh, ctx_len, HEAD_DIM), lambda b: (av, b, 0)),
                  pl.BlockSpec((kvh, 1, g * ctx_len), lambda b: (0, 0, 0))],
        out_specs=pl.BlockSpec((ctx_len, kvh * g * HEAD_DIM), lambda b: (b, 0)),
        out_shape=jax.ShapeDtypeStruct((batch * ctx_len, kvh * g * HEAD_DIM), BF16),
        compiler_params=_params("parallel"),
        name="context_attention",
    )(p_ctx, p_ctx, p_ctx, sink_row)


def _ret_body(lg_ref, q_ref, k_ref, v_ref, g_ref, *rest, n_lat, n_ctx, ctx_out):
    if ctx_out:
        cq_ref, ck_ref, cv_ref, cg_ref, gn_ref, o_ref, co_ref, yf_ref = rest
    else:
        ck_ref, cv_ref, gn_ref, o_ref, yf_ref = rest
        cq_ref = cg_ref = co_ref = None
    c = RET_CHUNK
    h = pl.program_id(1)
    lgf = lg_ref[0, h]
    lgb = lg_ref[1, h]
    ii = lax.broadcasted_iota(jnp.int32, (c, c), 0)
    jj = lax.broadcasted_iota(jnp.int32, (c, c), 1)
    rel = (ii - jj).astype(F32)
    dm_f = jnp.where(rel >= 0, jnp.exp(lgf * jnp.maximum(rel, 0.0)), 0.0)
    dm_b = jnp.where(rel < 0, jnp.exp(lgb * jnp.maximum(-rel, 0.0)), 0.0)
    pos = lax.broadcasted_iota(jnp.int32, (c, HEAD_DIM), 0).astype(F32)
    qd_f = jnp.exp(lgf * (pos + 1.0))
    kd_f = jnp.exp(lgf * (c - 1.0 - pos))
    qd_b = jnp.exp(lgb * (c - pos))
    kd_b = jnp.exp(lgb * pos)
    cd_f = jnp.exp(jnp.full((1, RET_V_DIM), lgf * c, F32))
    cd_b = jnp.exp(jnp.full((1, RET_V_DIM), lgb * c, F32))
    gn = gn_ref[0]

    def rows(ci):
        return slice(ci * c, (ci + 1) * c)

    def wide(ref, ci):
        return jnp.concatenate([ref[0, rows(ci), :], ref[1, rows(ci), :]], axis=1)

    def step(q, k, v, state, dm, qd, kd, cd):
        kv = lax.dot_general((k.astype(F32) * kd).astype(BF16), v, (((0,), (0,)), ((), ())),
                             preferred_element_type=F32)
        out = None
        if q is not None:
            s = lax.dot_general(q, k, (((1,), (1,)), ((), ())), preferred_element_type=F32) * dm
            out = jnp.dot(s.astype(BF16), v, preferred_element_type=F32)
            if state is not None:
                out = out + jnp.dot((q.astype(F32) * qd).astype(BF16), state.astype(BF16),
                                    preferred_element_type=F32)
        return out, (kv if state is None else state * cd + kv)

    def finish(y, gate):
        mu = jnp.mean(y, axis=-1, keepdims=True)
        dlt = y - mu
        var = jnp.mean(dlt * dlt, axis=-1, keepdims=True)
        yn = dlt * lax.rsqrt(var + EPS) * gn
        gf = gate.astype(F32)
        return ((gf * jax.nn.sigmoid(gf)) * yn).astype(BF16)

    n_y_ctx = n_ctx if ctx_out else 0

    state = None
    for ci in range(n_ctx):
        q = cq_ref[0, rows(ci), :] if ctx_out else None
        out, state = step(q, ck_ref[0, rows(ci), :], wide(cv_ref, ci), state, dm_f, qd_f, kd_f, cd_f)
        if ctx_out:
            yf_ref[rows(ci), :] = out
    for ci in range(n_lat):
        out, state = step(q_ref[0, rows(ci), :], k_ref[0, rows(ci), :], wide(v_ref, ci), state,
                          dm_f, qd_f, kd_f, cd_f)
        yf_ref[rows(n_y_ctx + ci), :] = out

    state = None
    for ci in reversed(range(n_ctx)):
        q = cq_ref[0, rows(ci), :] if ctx_out else None
        out, state = step(q, ck_ref[0, rows(ci), :], wide(cv_ref, ci), state, dm_b, qd_b, kd_b, cd_b)
        if ctx_out:
            co_ref[rows(ci), :] = finish(yf_ref[rows(ci), :] + out, wide(cg_ref, ci))
    for ci in reversed(range(n_lat)):
        out, state = step(q_ref[0, rows(ci), :], k_ref[0, rows(ci), :], wide(v_ref, ci), state,
                          dm_b, qd_b, kd_b, cd_b)
        o_ref[rows(ci), :] = finish(yf_ref[rows(n_y_ctx + ci), :] + out, wide(g_ref, ci))


def _retention(p_lat, p_ctx, log_decay, gn_g, layout, ctx_layout, batch, seq_len, ctx_len, ctx_out):
    nh = layout.ret_heads
    vchunks = RET_V_DIM // LANES
    narrow = lambda base, ln: pl.BlockSpec((1, ln, HEAD_DIM), lambda b, h, lg: (base + h, b, 0))
    wide = lambda base, ln: pl.BlockSpec((vchunks, ln, HEAD_DIM), lambda b, h, lg: (base // vchunks + h, b, 0))
    in_specs = [narrow(layout.rq0, seq_len), narrow(layout.rk0, seq_len),
                wide(layout.rv0, seq_len), wide(layout.rg0, seq_len)]
    args = [p_lat] * 4
    if ctx_out:
        in_specs += [narrow(ctx_layout.rq0, ctx_len), narrow(ctx_layout.rk0, ctx_len),
                     wide(ctx_layout.rv0, ctx_len), wide(ctx_layout.rg0, ctx_len)]
        args += [p_ctx] * 4
    else:
        in_specs += [narrow(ctx_layout.rk0, ctx_len), wide(ctx_layout.rv0, ctx_len)]
        args += [p_ctx] * 2
    in_specs.append(pl.BlockSpec((1, 1, RET_V_DIM), lambda b, h, lg: (h, 0, 0)))
    args.append(gn_g.reshape(nh, 1, RET_V_DIM))
    out_specs = [pl.BlockSpec((seq_len, RET_V_DIM), lambda b, h, lg: (b, h))]
    out_shape = [jax.ShapeDtypeStruct((batch * seq_len, nh * RET_V_DIM), BF16)]
    y_rows = seq_len
    if ctx_out:
        out_specs.append(pl.BlockSpec((ctx_len, RET_V_DIM), lambda b, h, lg: (b, h)))
        out_shape.append(jax.ShapeDtypeStruct((batch * ctx_len, nh * RET_V_DIM), BF16))
        y_rows += ctx_len
    outs = pl.pallas_call(
        functools.partial(_ret_body, n_lat=seq_len // RET_CHUNK, n_ctx=ctx_len // RET_CHUNK, ctx_out=ctx_out),
        grid_spec=pltpu.PrefetchScalarGridSpec(
            num_scalar_prefetch=1,
            grid=(batch, nh),
            in_specs=in_specs,
            out_specs=out_specs,
            scratch_shapes=[pltpu.VMEM((y_rows, RET_V_DIM), F32)]),
        out_shape=out_shape,
        compiler_params=_params("parallel", "parallel"),
        name="retention" if ctx_out else "retention_last",
    )(log_decay, *args)
    return (outs[0], outs[1]) if ctx_out else (outs[0], None)


def _residual_epilogue(xo_ref, x_ref, gt_ref, gpost_ref, nxt, part_ref, tm):
    d = xo_ref.shape[1]
    nc = d // LANES
    blk = (BF16_ROWS // SUBLANES, SUBLANES, LANES)
    n_blocks = tm // BF16_ROWS

    def rows_of(rb):
        return pl.ds(pl.multiple_of(rb * BF16_ROWS, BF16_ROWS), BF16_ROWS)

    def cols(c):
        return slice(c * LANES, (c + 1) * LANES)

    def partials_to_rsqrt():
        ss = jnp.sum(part_ref[...], axis=-1, keepdims=True)
        part_ref[...] = jnp.broadcast_to(lax.rsqrt(ss / d + EPS), part_ref.shape)

    def sweep_sumsq(rb, carry):
        rws = rows_of(rb)
        acc = [jnp.zeros((BF16_ROWS, LANES), F32), jnp.zeros((BF16_ROWS, LANES), F32)]
        for c in range(nc):
            y = xo_ref[rws, cols(c)]
            acc[c % 2] = acc[c % 2] + y * y
        part_ref[rws, :] = acc[0] + acc[1]
        return carry

    def sweep_residual(rb, carry):
        rws = rows_of(rb)
        r = part_ref[rws, :].reshape(blk)
        acc = [jnp.zeros(blk, F32), jnp.zeros(blk, F32)]
        for c in range(nc):
            y = xo_ref[rws, cols(c)].reshape(blk)
            xn = x_ref[rws, cols(c)].reshape(blk) + gt_ref[:, :, cols(c)] * ((y * r) * gpost_ref[:, cols(c)][None])
            xo_ref[rws, cols(c)] = xn.reshape(BF16_ROWS, LANES)
            acc[c % 2] = acc[c % 2] + xn * xn
        part_ref[rws, :] = (acc[0] + acc[1]).reshape(BF16_ROWS, LANES)
        return carry

    lax.fori_loop(0, n_blocks, sweep_sumsq, 0, unroll=2)
    partials_to_rsqrt()
    lax.fori_loop(0, n_blocks, sweep_residual, 0, unroll=2)
    if nxt is None:
        return
    gpre_ref, sh_ref, sc_ref, hn_ref = nxt
    sc1 = 1.0 + sc_ref[...]

    def sweep_prenorm(rb, carry):
        rws = rows_of(rb)
        r2 = part_ref[rws, :].reshape(blk)
        for c in range(nc):
            xn = xo_ref[rws, cols(c)].reshape(blk)
            hn = ((xn * r2) * gpre_ref[:, cols(c)][None]) * sc1[:, :, cols(c)] + sh_ref[:, :, cols(c)]
            hn_ref[rws, cols(c)] = hn.reshape(BF16_ROWS, LANES).astype(BF16)
        return carry

    partials_to_rsqrt()
    lax.fori_loop(0, n_blocks, sweep_prenorm, 0, unroll=2)


def _outproj_body(a_ref, r_ref, w_hbm, x_ref, gt_ref, gpost_ref, gpre_ref, sh_ref, sc_ref,
                  xo_ref, hn_ref, part_ref, wbuf, sem, *, nj, tn, tm, li, n_total):
    j = pl.program_id(1)
    t = pl.program_id(0) * nj + j

    def weight_copy(step):
        cols = pl.ds(pl.multiple_of((step % nj) * tn, tn), tn)
        slot = step % W_SLOTS
        return pltpu.make_async_copy(w_hbm.at[li, :, cols], wbuf.at[slot], sem.at[slot])

    @pl.when(t == 0)
    def _():
        weight_copy(t).start()
        weight_copy(t + 1).start()

    @pl.when(t + 2 < n_total)
    def _():
        weight_copy(t + 2).start()

    weight_copy(t).wait()
    ka = a_ref.shape[1]
    slot = t % W_SLOTS
    y = (jnp.dot(a_ref[...], wbuf[slot, :ka, :], preferred_element_type=F32)
         + jnp.dot(r_ref[...], wbuf[slot, ka:, :], preferred_element_type=F32))
    for jj in range(nj):
        @pl.when(j == jj)
        def _():
            xo_ref[:, jj * tn:(jj + 1) * tn] = y

    @pl.when(j == nj - 1)
    def _():
        _residual_epilogue(xo_ref, x_ref, gt_ref, gpost_ref, (gpre_ref, sh_ref, sc_ref, hn_ref), part_ref, tm)


def _resident(shape, index_map):
    return pl.BlockSpec(shape, index_map, pipeline_mode=pl.Buffered(1))


def _outproj(att, ret, w_out, li, x, gt, g_post, g_pre, sh, sc, group_of_tile, tm=512, tn=512):
    m, d = x.shape
    ka = att.shape[1]
    assert ret.shape[1] == ka and w_out.shape[1] == 2 * ka
    nj = d // tn
    mod_spec = pl.BlockSpec((1, SUBLANES, d), lambda i, j: (group_of_tile(i, tm), 0, 0))
    vec_spec = pl.BlockSpec((SUBLANES, d), lambda i, j: (0, 0))
    row_tile = lambda i, j: (i, 0)
    n_total = (m // tm) * nj
    assert n_total >= 2
    return pl.pallas_call(
        functools.partial(_outproj_body, nj=nj, tn=tn, tm=tm, li=li, n_total=n_total),
        grid=(m // tm, nj),
        in_specs=[pl.BlockSpec((tm, ka), row_tile), pl.BlockSpec((tm, ka), row_tile),
                  pl.BlockSpec(memory_space=pl.ANY),
                  _resident((tm, d), row_tile), mod_spec, vec_spec, vec_spec, mod_spec, mod_spec],
        out_specs=[pl.BlockSpec((tm, d), row_tile), pl.BlockSpec((tm, d), row_tile)],
        out_shape=[jax.ShapeDtypeStruct((m, d), F32), jax.ShapeDtypeStruct((m, d), BF16)],
        scratch_shapes=[pltpu.VMEM((tm, LANES), F32), pltpu.VMEM((W_SLOTS, 2 * ka, tn), BF16),
                        pltpu.SemaphoreType.DMA((W_SLOTS,))],
        compiler_params=_params("arbitrary", "arbitrary"),
        name="outproj",
    )(att, ret, w_out, x, gt, _rep8(g_post), _rep8(g_pre), sh, sc)


def _mlp_body(h_ref, wu_ref, wd_ref, x_ref, gt_ref, gpost_ref, *rest, nf, nt, tm, has_next):
    if has_next:
        gpre_ref, sh_ref, sc_ref, xo_ref, hn_ref, part_ref = rest
        nxt = (gpre_ref, sh_ref, sc_ref, hn_ref)
    else:
        xo_ref, part_ref = rest
        nxt = None
    f = pl.program_id(1)

    @pl.when(f == 0)
    def _():
        xo_ref[...] = jnp.zeros_like(xo_ref)

    u = jnp.dot(h_ref[...], wu_ref[0], preferred_element_type=F32)
    a = jnp.maximum(u, 0.0)
    a = (a * a).astype(BF16)
    for nn in range(xo_ref.shape[1] // nt):
        cols = slice(nn * nt, (nn + 1) * nt)
        xo_ref[:, cols] += jnp.dot(a, wd_ref[0, :, cols], preferred_element_type=F32)

    @pl.when(f == nf - 1)
    def _():
        _residual_epilogue(xo_ref, x_ref, gt_ref, gpost_ref, nxt, part_ref, tm)


def _mlp(h, w_up, w_down, li, x, gt, g_post, nxt, group_of_tile, tm=512, tf=512, nt=512):
    m, d = x.shape
    ff = w_up.shape[2]
    nf = ff // tf
    mod_spec = pl.BlockSpec((1, SUBLANES, d), lambda i, f: (group_of_tile(i, tm), 0, 0))
    vec_spec = pl.BlockSpec((SUBLANES, d), lambda i, f: (0, 0))
    row_tile = lambda i, f: (i, 0)
    in_specs = [_resident((tm, d), row_tile),
                pl.BlockSpec((1, d, tf), lambda i, f: (li, 0, f)),
                pl.BlockSpec((1, tf, d), lambda i, f: (li, f, 0)),
                _resident((tm, d), row_tile), mod_spec, vec_spec]
    args = [h, w_up, w_down, x, gt, _rep8(g_post)]
    out_specs = [pl.BlockSpec((tm, d), row_tile)]
    out_shape = [jax.ShapeDtypeStruct((m, d), F32)]
    if nxt is not None:
        g_pre, sh, sc = nxt
        in_specs += [vec_spec, mod_spec, mod_spec]
        args += [_rep8(g_pre), sh, sc]
        out_specs.append(pl.BlockSpec((tm, d), row_tile))
        out_shape.append(jax.ShapeDtypeStruct((m, d), BF16))
    outs = pl.pallas_call(
        functools.partial(_mlp_body, nf=nf, nt=nt, tm=tm, has_next=nxt is not None),
        grid=(m // tm, nf),
        in_specs=in_specs,
        out_specs=out_specs,
        out_shape=out_shape,
        scratch_shapes=[pltpu.VMEM((tm, LANES), F32)],
        compiler_params=_params("parallel", "arbitrary"),
        name="mlp" if nxt is not None else "mlp_last",
    )(*args)
    return (outs[0], outs[1]) if nxt is not None else (outs[0], None)


class _Layout:
    def __init__(self, d_model):
        self.attn_heads = d_model // (2 * HEAD_DIM)
        self.kv_heads = self.attn_heads // ATTN_GROUP
        self.ret_heads = d_model // (2 * RET_V_DIM)
        vchunks = RET_V_DIM // LANES
        self.aq0 = 0
        self.ak0 = self.aq0 + self.attn_heads
        self.av0 = self.ak0 + self.kv_heads
        self.rq0 = self.av0 + self.kv_heads
        self.rk0 = self.rq0 + self.ret_heads
        self.rv0 = self.rk0 + self.ret_heads
        self.rg0 = self.rv0 + vchunks * self.ret_heads
        self.n_chunks = self.rg0 + vchunks * self.ret_heads

    def key_value_tiles(self, n_chunk):
        chunks = list(range(self.ak0, self.rq0)) + list(range(self.rk0, self.rg0))
        return tuple(sorted({c // n_chunk for c in chunks}))

    def restricted_to(self, tiles, n_chunk):
        sub = _Layout.__new__(_Layout)
        sub.kv_heads, sub.ret_heads = self.kv_heads, self.ret_heads
        where = lambda c: tiles.index(c // n_chunk) * n_chunk + c % n_chunk
        sub.ak0, sub.av0, sub.rk0, sub.rv0 = (where(c) for c in (self.ak0, self.av0, self.rk0, self.rv0))
        return sub

    def kind_of_chunk(self, c):
        if c < self.ak0:
            return "q"
        if c < self.av0:
            return "rope"
        if c < self.rq0:
            return "none"
        if c < self.rk0:
            return "rope"
        if c < self.rv0:
            return "rk"
        return "none"


def _rope_tables(seq_len):
    pairs = HEAD_DIM // 4
    t = jnp.arange(seq_len)
    r = (t // GRID_W).astype(F32)
    cl = (t % GRID_W).astype(F32)
    inv = ROPE_BASE ** (-jnp.arange(pairs, dtype=F32) / pairs)
    ar = r[:, None] * inv
    ac = cl[:, None] * inv
    ang = jnp.concatenate([ar, ar, ac, ac], axis=-1)
    cos, sin = jnp.cos(ang), jnp.sin(ang)
    first = (jnp.arange(HEAD_DIM) % (2 * pairs)) < pairs
    return cos, jnp.where(first, -sin, 0.0), jnp.where(first, 0.0, sin)


def kernel(x, c, ctx, c_ctx, w_ada, b_ada, norm_g, w_in, attn_sink, ret_log_decay, ret_gn_g, w_out, w_up, w_down):
    batch, seq_len, d = x.shape
    ctx_len = ctx.shape[1]
    depth = w_in.shape[0]
    layout = _Layout(d)
    assert layout.n_chunks * LANES == w_in.shape[2]
    assert batch + 1 <= MOD_ROWS

    xf = x.reshape(batch * seq_len, d)
    cf = ctx.reshape(batch * ctx_len, d)

    c_all = jnp.concatenate([c, c_ctx[None, :], jnp.zeros((MOD_ROWS - batch - 1, d), F32)], axis=0)
    mod = _adaln(c_all, w_ada, b_ada)
    mod = mod.reshape(depth, MOD_ROWS, 6, 1, d).transpose(0, 2, 1, 3, 4)
    mod = jnp.broadcast_to(mod, (depth, 6, MOD_ROWS, SUBLANES, d))
    lat_group = lambda i, tm: (i * tm) // seq_len
    ctx_group = lambda i, tm: batch

    rope_tabs = _rope_tables(seq_len)
    g = ATTN_GROUP
    wi, wo, wu, wd = (w.astype(BF16) for w in (w_in, w_out, w_up, w_down))

    h = _prenorm(xf, norm_g[0, 0], mod[0, 0], mod[0, 1], lat_group)
    hc = _prenorm(cf, norm_g[0, 0], mod[0, 0], mod[0, 1], ctx_group)
    for li in range(depth):
        last = li == depth - 1
        sh1, sc1, gt1, sh2, sc2, gt2 = [mod[li, k] for k in range(6)]
        g_pre_mix, g_post_mix, g_pre_mlp, g_post_mlp = [norm_g[li, k] for k in range(4)]
        sink = attn_sink[li].reshape(layout.kv_heads, g)

        p_lat = _inproj(h, wi, li, layout, rope_tabs, seq_len)
        if last:
            kv_tiles = layout.key_value_tiles(IN_TN // LANES)
            ctx_layout = layout.restricted_to(kv_tiles, IN_TN // LANES)
            p_ctx = _inproj(hc, wi, li, layout, None, ctx_len, tiles=kv_tiles)
        else:
            ctx_layout = layout
            p_ctx = _inproj(hc, wi, li, layout, None, ctx_len)
        sink_lat = jnp.repeat(sink, ATTN_BLOCK, axis=1)[:, None, :]
        att = _attention(p_lat, p_ctx, sink_lat, layout, ctx_layout, batch, seq_len, ctx_len)
        ret, cret = _retention(p_lat, p_ctx, ret_log_decay[li], ret_gn_g[li], layout, ctx_layout,
                               batch, seq_len, ctx_len, ctx_out=not last)
        xf, h2 = _outproj(att, ret, wo, li, xf, gt1, g_post_mix, g_pre_mlp, sh2, sc2, lat_group)
        nxt = None if last else (norm_g[li + 1, 0], mod[li + 1, 0], mod[li + 1, 1])
        xf, h = _mlp(h2, wu, wd, li, xf, gt2, g_post_mlp, nxt, lat_group)
        if not last:
            sink_ctx = jnp.repeat(sink, ctx_len, axis=1)[:, None, :]
            catt = _ctx_attention(p_ctx, sink_ctx, layout, batch, ctx_len)
            cf, hc2 = _outproj(catt, cret, wo, li, cf, gt1, g_post_mix, g_pre_mlp, sh2, sc2, ctx_group)
            cf, hc = _mlp(hc2, wu, wd, li, cf, gt2, g_post_mlp, nxt, ctx_group)
    return xf.reshape(batch, seq_len, d)
```
